```python
import jax, jax.numpy as jnp
from jax import lax
import numpy as np

D_MODEL = 1024
BATCH = 2
SEQ = 8192
DEPTH = 4

GRID_W = 64
CTX_LEN = 256
EPS = 1e-6
NEG_INF = -1e30

NA_HEADS = 8
NA_HEAD_DIM = 64
NA_WIDTH = NA_HEADS * NA_HEAD_DIM
NA_KH = 8
NA_KW = 16
SGU_GROUPS = 8
SGU_WIDTH = D_MODEL - NA_WIDTH
SGU_GROUP_DIM = SGU_WIDTH // SGU_GROUPS
SGU_CHUNK = 128
HYB_IN = 3 * NA_WIDTH + 2 * SGU_WIDTH

RET_HEADS = 8
RET_QK_DIM = D_MODEL // RET_HEADS
RET_V_DIM = 2 * RET_QK_DIM
RET_QK_WIDTH = RET_HEADS * RET_QK_DIM
RET_V_WIDTH = RET_HEADS * RET_V_DIM
RET_CHUNK = 128
RET_IN = 2 * RET_QK_WIDTH + 2 * RET_V_WIDTH
ROPE_BASE = 10000.0

D_FF = 2816
CONV_W = 3

N_EVEN = (DEPTH + 1) // 2
N_ODD = DEPTH // 2

kernel_name = "hybrid_na_sgu_retention_dit_trunk"


def rmsnorm(x, g):
    xf = x.astype(jnp.float32)
    y = xf * lax.rsqrt(jnp.mean(xf * xf, axis=-1, keepdims=True) + EPS)
    return (y * g.astype(jnp.float32)).astype(x.dtype)


def modulate(x, shift, scale):
    return x * (1 + scale) + shift


def axial_rope(n_tokens, head_dim):
    t = jnp.arange(n_tokens)
    row = (t // GRID_W).astype(jnp.float32)
    col = (t % GRID_W).astype(jnp.float32)
    n_freq = head_dim // 4
    inv = ROPE_BASE ** (-jnp.arange(n_freq, dtype=jnp.float32) / n_freq)
    ang = jnp.concatenate([row[:, None] * inv, col[:, None] * inv], axis=-1)
    return jnp.cos(ang), jnp.sin(ang)


def apply_rope(x, cos, sin):
    x1, x2 = jnp.split(x, 2, axis=-1)
    c = cos[None, :, None, :]
    s = sin[None, :, None, :]
    return jnp.concatenate([x1 * c - x2 * s, x1 * s + x2 * c], axis=-1).astype(x.dtype)


def neighbourhood_attention(q, k, v, k_ctx, v_ctx, rpb):
    B, T, H, dh = q.shape
    rows = T // GRID_W
    kh = min(NA_KH, rows)
    kw = NA_KW
    scale = dh ** -0.5
    qg = q.reshape(B, rows, GRID_W, H, dh)
    kg = k.reshape(B, rows, GRID_W, H, dh)
    vg = v.reshape(B, rows, GRID_W, H, dh)
    r = jnp.arange(rows)
    r0 = jnp.clip(r - kh // 2, 0, rows - kh)
    key_rows = r0[:, None] + jnp.arange(kh)[None, :]
    k_blk = kg[:, key_rows].reshape(B, rows, kh * GRID_W, H, dh)
    v_blk = vg[:, key_rows].reshape(B, rows, kh * GRID_W, H, dh)
    s_nb = jnp.einsum('brqhd,brkhd->bhrqk', qg, k_blk,
                      preferred_element_type=jnp.float32) * scale
    cidx = jnp.arange(GRID_W)
    c0 = jnp.clip(cidx - kw // 2, 0, GRID_W - kw)
    in_win = (cidx[None, :] >= c0[:, None]) & (cidx[None, :] < c0[:, None] + kw)
    mask = jnp.broadcast_to(in_win[:, None, :], (GRID_W, kh, GRID_W)).reshape(GRID_W, kh * GRID_W)
    dr = key_rows - r[:, None] + (NA_KH - 1)
    dc = jnp.clip(cidx[None, :] - cidx[:, None], -(kw - 1), kw - 1) + (kw - 1)
    bias = rpb[:, dr[:, None, :, None], dc[None, :, None, :]]
    bias = bias.reshape(H, rows, GRID_W, kh * GRID_W).astype(jnp.float32)
    s_nb = jnp.where(mask, s_nb + bias[None], NEG_INF)
    s_cx = jnp.einsum('brqhd,bchd->bhrqc', qg, k_ctx,
                      preferred_element_type=jnp.float32) * scale
    p = jax.nn.softmax(jnp.concatenate([s_nb, s_cx], axis=-1), axis=-1)
    n_nb = kh * GRID_W
    p_nb = p[..., :n_nb].astype(v.dtype)
    p_cx = p[..., n_nb:].astype(v.dtype)
    o = (jnp.einsum('bhrqk,brkhd->brqhd', p_nb, v_blk)
         + jnp.einsum('bhrqc,bchd->brqhd', p_cx, v_ctx))
    return o.reshape(B, T, H * dh).astype(q.dtype)


def context_attention(q, k, v):
    B, L, H, dh = q.shape
    s = jnp.einsum('bqhd,bkhd->bhqk', q, k, preferred_element_type=jnp.float32) * dh ** -0.5
    p = jax.nn.softmax(s, axis=-1).astype(v.dtype)
    return jnp.einsum('bhqk,bkhd->bqhd', p, v).reshape(B, L, H * dh).astype(q.dtype)


def spatial_gating(u, v, w_s, b_s):
    B, N, _ = v.shape
    vf = v.astype(jnp.float32)
    mu = jnp.mean(vf, axis=-1, keepdims=True)
    var = jnp.mean(jnp.square(vf - mu), axis=-1, keepdims=True)
    vn = ((vf - mu) * lax.rsqrt(var + EPS)).astype(v.dtype)
    vc = vn.reshape(B, N // SGU_CHUNK, SGU_CHUNK, SGU_GROUPS, SGU_GROUP_DIM)
    mixed = jnp.einsum('gpq,bnqgc->bnpgc', w_s, vc) + b_s.T[None, None, :, :, None]
    return u * mixed.reshape(B, N, SGU_WIDTH)


def hybrid_na_sgu(h_ctx, h_lat, w_in, rpb, w_s, b_s, w_out, ctx_out):
    heads = lambda a: a.reshape(a.shape[0], a.shape[1], NA_HEADS, NA_HEAD_DIM)
    splits = [NA_WIDTH, 2 * NA_WIDTH, 3 * NA_WIDTH, 3 * NA_WIDTH + SGU_WIDTH]
    q, k, v, u, g = jnp.split(h_lat @ w_in, splits, axis=-1)
    if ctx_out:
        qc, kc, vc, uc, gc = jnp.split(h_ctx @ w_in, splits, axis=-1)
    else:
        kc, vc = jnp.split(h_ctx @ w_in[:, NA_WIDTH:3 * NA_WIDTH], 2, axis=-1)
    a_lat = neighbourhood_attention(heads(q), heads(k), heads(v), heads(kc), heads(vc), rpb)
    s_lat = spatial_gating(jax.nn.gelu(u), jax.nn.gelu(g), w_s, b_s)
    y_lat = jnp.concatenate([a_lat, s_lat], axis=-1) @ w_out
    y_ctx = None
    if ctx_out:
        a_ctx = context_attention(heads(qc), heads(kc), heads(vc))
        s_ctx = spatial_gating(jax.nn.gelu(uc), jax.nn.gelu(gc), w_s, b_s)
        y_ctx = jnp.concatenate([a_ctx, s_ctx], axis=-1) @ w_out
    return y_ctx, y_lat


def retention_chunked(q, k, v, lg, state0):
    B, N, H, dk = q.shape
    dv = v.shape[-1]
    C = RET_CHUNK
    n = N // C
    to_chunks = lambda a: a.reshape(B, n, C, H, a.shape[-1]).transpose(1, 0, 3, 2, 4)
    pos = jnp.arange(C, dtype=jnp.float32)
    diff = pos[:, None] - pos[None, :]
    lower = diff >= 0
    decay_intra = jnp.where(lower[None], jnp.exp(lg[:, None, None] * jnp.where(lower, diff, 0.0)[None]), 0.0)
    decay_q = jnp.exp(lg[:, None] * (pos + 1.0))[None, :, :, None]
    decay_k = jnp.exp(lg[:, None] * (C - 1.0 - pos))[None, :, :, None]
    decay_chunk = jnp.exp(lg * C)[None, :, None, None]

    def step(S, blk):
        qb, kb, vb = blk
        qf = qb.astype(jnp.float32)
        kf = kb.astype(jnp.float32)
        vf = vb.astype(jnp.float32)
        s = jnp.einsum('bhqd,bhkd->bhqk', qf, kf) * decay_intra
        o = jnp.einsum('bhqk,bhkv->bhqv', s, vf) + jnp.einsum('bhqd,bhdv->bhqv', qf * decay_q, S)
        S = S * decay_chunk + jnp.einsum('bhkd,bhkv->bhdv', kf * decay_k, vf)
        return S, o

    _, o = lax.scan(step, state0, (to_chunks(q), to_chunks(k), to_chunks(v)))
    return o.transpose(1, 0, 3, 2, 4).reshape(B, N, H, dv)


def retention_final_state(k, v, lg):
    L = k.shape[1]
    w = jnp.exp(lg[None, :] * (L - 1.0 - jnp.arange(L, dtype=jnp.float32))[:, None])
    return jnp.einsum('blhd,blhv->bhdv', k.astype(jnp.float32) * w[None, :, :, None], v.astype(jnp.float32))


def retention_output(o, g, gn_g, w_out):
    B, N = o.shape[0], o.shape[1]
    mu = jnp.mean(o, axis=-1, keepdims=True)
    var = jnp.mean(jnp.square(o - mu), axis=-1, keepdims=True)
    on = ((o - mu) * lax.rsqrt(var + EPS)).reshape(B, N, RET_V_WIDTH) * gn_g.astype(jnp.float32)
    return (jax.nn.silu(g) * on.astype(g.dtype)) @ w_out


def retention_mixer(h_ctx, h_lat, w_in, log_decay, gn_g, w_out, ctx_out):
    B, T, _ = h_lat.shape
    L = h_ctx.shape[1]
    H, dk, dv = RET_HEADS, RET_QK_DIM, RET_V_DIM
    splits = [RET_QK_WIDTH, 2 * RET_QK_WIDTH, 2 * RET_QK_WIDTH + RET_V_WIDTH]
    q, k, v, g = jnp.split(h_lat @ w_in, splits, axis=-1)
    cos, sin = axial_rope(T, dk)
    q = apply_rope(q.reshape(B, T, H, dk), cos, sin)
    k = apply_rope(k.reshape(B, T, H, dk) * dk ** -0.5, cos, sin)
    v = v.reshape(B, T, H, dv)
    if ctx_out:
        qc, kc, vc, gc = jnp.split(h_ctx @ w_in, splits, axis=-1)
        qc = qc.reshape(B, L, H, dk)
    else:
        kc, vc = jnp.split(h_ctx @ w_in[:, RET_QK_WIDTH:2 * RET_QK_WIDTH + RET_V_WIDTH], [RET_QK_WIDTH], axis=-1)
    kc = kc.reshape(B, L, H, dk) * dk ** -0.5
    vc = vc.reshape(B, L, H, dv)
    outs_lat = []
    outs_ctx = []
    for d in range(2):
        lg = log_decay[d].astype(jnp.float32)
        rev = (lambda a: a[:, ::-1]) if d == 1 else (lambda a: a)
        s_ctx = retention_final_state(rev(kc), rev(vc), lg)
        outs_lat.append(rev(retention_chunked(rev(q), rev(k), rev(v), lg, s_ctx)))
        if ctx_out:
            outs_ctx.append(rev(retention_chunked(rev(qc), rev(kc), rev(vc), lg, jnp.zeros_like(s_ctx))))
    y_lat = retention_output(outs_lat[0] + outs_lat[1], g, gn_g, w_out)
    y_ctx = None
    if ctx_out:
        y_ctx = retention_output(outs_ctx[0] + outs_ctx[1], gc, gn_g, w_out)
    return y_ctx, y_lat


def dwconv3(x, w, b):
    xp = jnp.pad(x, ((0, 0), (1, 1), (0, 0)))
    return xp[:, :-2] * w[0] + xp[:, 1:-1] * w[1] + xp[:, 2:] * w[2] + b


def conv_ffn(h, w_in, conv_w, conv_b, w_out):
    a = dwconv3(h @ w_in, conv_w, conv_b)
    gate, up = jnp.split(a, 2, axis=-1)
    return (jax.nn.silu(gate) * up) @ w_out


def setup_inputs(seed: int = 0) -> dict:
    key = jax.random.key(seed)
    ks = jax.random.split(key, 20)
    nrm = lambda k, shape, s: jax.random.normal(k, shape, jnp.float32) * s
    base_decay = np.log1p(-2.0 ** (-5.0 - np.arange(RET_HEADS))).astype(np.float32)
    return {
        "x": nrm(ks[0], (BATCH, SEQ, D_MODEL), 1.0),
        "c": nrm(ks[1], (BATCH, D_MODEL), 1.0),
        "ctx": nrm(ks[2], (BATCH, CTX_LEN, D_MODEL), 1.0),
        "c_ctx": nrm(ks[3], (D_MODEL,), 1.0),
        "ada_w": nrm(ks[4], (DEPTH, D_MODEL, 6 * D_MODEL), 0.5 * D_MODEL ** -0.5),
        "ada_b": nrm(ks[5], (DEPTH, 6 * D_MODEL), 0.02),
        "norm_g": 1.0 + nrm(ks[6], (DEPTH, 4, D_MODEL), 0.1),
        "hyb_w_in": nrm(ks[7], (N_EVEN, D_MODEL, HYB_IN), D_MODEL ** -0.5),
        "na_rpb": nrm(ks[8], (N_EVEN, NA_HEADS, 2 * NA_KH - 1, 2 * NA_KW - 1), 0.1),
        "sgu_w": nrm(ks[9], (N_EVEN, SGU_GROUPS, SGU_CHUNK, SGU_CHUNK), SGU_CHUNK ** -0.5),
        "sgu_b": 1.0 + nrm(ks[10], (N_EVEN, SGU_GROUPS, SGU_CHUNK), 0.1),
        "hyb_w_out": nrm(ks[11], (N_EVEN, D_MODEL, D_MODEL), D_MODEL ** -0.5),
        "ret_w_in": nrm(ks[12], (N_ODD, D_MODEL, RET_IN), D_MODEL ** -0.5),
        "ret_log_decay": jnp.asarray(base_decay)[None, None, :] * (1.0 + nrm(ks[13], (N_ODD, 2, RET_HEADS), 0.05)),
        "ret_gn_g": 1.0 + nrm(ks[14], (N_ODD, RET_V_WIDTH), 0.1),
        "ret_w_out": nrm(ks[15], (N_ODD, RET_V_WIDTH, D_MODEL), RET_V_WIDTH ** -0.5),
        "ffn_w_in": nrm(ks[16], (DEPTH, D_MODEL, 2 * D_FF), D_MODEL ** -0.5),
        "ffn_conv_w": nrm(ks[17], (DEPTH, CONV_W, 2 * D_FF), CONV_W ** -0.5),
        "ffn_conv_b": nrm(ks[18], (DEPTH, 2 * D_FF), 0.02),
        "ffn_w_out": nrm(ks[19], (DEPTH, D_FF, D_MODEL), D_FF ** -0.5),
    }


def reference(x, c, ctx, c_ctx, ada_w, ada_b, norm_g, hyb_w_in, na_rpb, sgu_w, sgu_b, hyb_w_out,
              ret_w_in, ret_log_decay, ret_gn_g, ret_w_out, ffn_w_in, ffn_conv_w, ffn_conv_b, ffn_w_out):
    xc = ctx
    silu_c = jax.nn.silu(c)
    silu_cc = jax.nn.silu(c_ctx)
    for i in range(DEPTH):
        ctx_out = i < DEPTH - 1
        j = i // 2
        mod_lat = (silu_c @ ada_w[i] + ada_b[i])[:, None, :]
        mod_ctx = (silu_cc @ ada_w[i] + ada_b[i])[None, None, :]
        sh1, sc1, g1, sh2, sc2, g2 = jnp.split(mod_lat, 6, axis=-1)
        csh1, csc1, cg1, csh2, csc2, cg2 = jnp.split(mod_ctx, 6, axis=-1)
        h_lat = modulate(rmsnorm(x, norm_g[i, 0]), sh1, sc1)
        h_ctx = modulate(rmsnorm(xc, norm_g[i, 0]), csh1, csc1)
        if i % 2 == 0:
            y_ctx, y_lat = hybrid_na_sgu(h_ctx, h_lat, hyb_w_in[j], na_rpb[j], sgu_w[j], sgu_b[j],
                                         hyb_w_out[j], ctx_out)
        else:
            y_ctx, y_lat = retention_mixer(h_ctx, h_lat, ret_w_in[j], ret_log_decay[j], ret_gn_g[j],
                                           ret_w_out[j], ctx_out)
        x = x + g1 * rmsnorm(y_lat, norm_g[i, 1])
        h = modulate(rmsnorm(x, norm_g[i, 2]), sh2, sc2)
        x = x + g2 * rmsnorm(conv_ffn(h, ffn_w_in[i], ffn_conv_w[i], ffn_conv_b[i], ffn_w_out[i]), norm_g[i, 3])
        if ctx_out:
            xc = xc + cg1 * rmsnorm(y_ctx, norm_g[i, 1])
            hc = modulate(rmsnorm(xc, norm_g[i, 2]), csh2, csc2)
            xc = xc + cg2 * rmsnorm(conv_ffn(hc, ffn_w_in[i], ffn_conv_w[i], ffn_conv_b[i], ffn_w_out[i]), norm_g[i, 3])
    return x
```

```python
import functools

import jax
import jax.numpy as jnp
from jax import lax
from jax.experimental import pallas as pl
from jax.experimental.pallas import tpu as pltpu

F32 = jnp.float32
BF16 = jnp.bfloat16

GRID_W = 64
EPS = 1e-6
NEG_INF = -1e30

NA_HEADS = 8
NA_HEAD_DIM = 64
NA_WIDTH = NA_HEADS * NA_HEAD_DIM
NA_KH = 8
NA_KW = 16
SGU_GROUPS = 8
SGU_CHUNK = 128

RET_HEADS = 8
RET_CHUNK = 128
ROPE_BASE = 10000.0

CONV_W = 3
FFN_CHUNK = 256
HALO = 16

V7X_VMEM_LIMIT_BYTES = 56 * 1024 * 1024

NT_DIMS = (((1,), (1,)), ((), ()))
TN_DIMS = (((0,), (0,)), ((), ()))


def _params(n_axes):
    return pltpu.CompilerParams(dimension_semantics=("arbitrary",) * n_axes,
                                vmem_limit_bytes=V7X_VMEM_LIMIT_BYTES)


def _resident(shape):
    zeros = (0,) * len(shape)
    return pl.BlockSpec(shape, lambda *_: zeros, pipeline_mode=pl.Buffered(1))


def _token_tile(n, want):
    tm = min(n, want)
    assert n % tm == 0
    return tm


def _rms(v, gain):
    return v * lax.rsqrt(jnp.mean(v * v, axis=-1, keepdims=True) + EPS) * gain


def _silu(v):
    return v * (1.0 / (1.0 + jnp.exp(-v)))


def _norm_mod(x, gain, shift, scale):
    return _rms(x, gain) * (1.0 + scale) + shift


def _ada_kernel(c_ref, w_ref, b_ref, o_ref):
    s = _silu(c_ref[...])
    o_ref[0] = jnp.dot(s, w_ref[0], preferred_element_type=F32) + b_ref[0]


def _ada_rows(cond, ada_w, ada_b):
    depth, d, n = ada_w.shape
    tn = 2048
    return pl.pallas_call(
        _ada_kernel,
        out_shape=jax.ShapeDtypeStruct((depth, 8, n), F32),
        grid=(depth, n // tn),
        in_specs=[pl.BlockSpec((8, d), lambda i, j: (0, 0)),
                  pl.BlockSpec((1, d, tn), lambda i, j: (i, 0, j)),
                  pl.BlockSpec((1, 1, tn), lambda i, j: (i, 0, j))],
        out_specs=pl.BlockSpec((1, 8, tn), lambda i, j: (i, 0, j)),
        compiler_params=_params(2),
        name="ada_rows",
    )(cond, ada_w, ada_b.reshape(depth, 1, n))


def _hyb_in_kernel(x_ref, sh_ref, sc_ref, g_ref, w_ref, qkv_ref, u_ref, vn_ref):
    hb = _norm_mod(x_ref[0], g_ref[...], sh_ref[0], sc_ref[0]).astype(BF16)
    w = NA_WIDTH
    for part in range(3):
        acc = jnp.dot(hb, w_ref[:, part * w:(part + 1) * w], preferred_element_type=F32)
        if part == 0:
            acc = acc * (NA_HEAD_DIM ** -0.5)
        qkv_ref[0, :, part * w:(part + 1) * w] = acc.astype(BF16)
    u = jax.nn.gelu(jnp.dot(hb, w_ref[:, 3 * w:4 * w], preferred_element_type=F32))
    u_ref[0] = u.astype(BF16)
    g = jax.nn.gelu(jnp.dot(hb, w_ref[:, 4 * w:5 * w], preferred_element_type=F32))
    mu = jnp.mean(g, axis=-1, keepdims=True)
    gc = g - mu
    var = jnp.mean(gc * gc, axis=-1, keepdims=True)
    vn_ref[0] = (gc * lax.rsqrt(var + EPS)).astype(BF16)


def _hyb_in(x, shift, scale, gain, w_bf):
    b, n, d = x.shape
    tm = _token_tile(n, 512)
    w = NA_WIDTH
    row = lambda bi, i: (bi, 0, 0)
    tile = lambda bi, i: (bi, i, 0)
    return pl.pallas_call(
        _hyb_in_kernel,
        out_shape=(jax.ShapeDtypeStruct((b, n, 3 * w), BF16),
                   jax.ShapeDtypeStruct((b, n, w), BF16),
                   jax.ShapeDtypeStruct((b, n, w), BF16)),
        grid=(b, n // tm),
        in_specs=[pl.BlockSpec((1, tm, d), tile),
                  pl.BlockSpec((1, 1, d), row),
                  pl.BlockSpec((1, 1, d), row),
                  _resident((1, d)),
                  _resident(w_bf.shape)],
        out_specs=(pl.BlockSpec((1, tm, 3 * w), tile),
                   pl.BlockSpec((1, tm, w), tile),
                   pl.BlockSpec((1, tm, w), tile)),
        compiler_params=_params(2),
        name="hyb_in",
    )(x, shift, scale, gain, w_bf)


def _split_heads(q2):
    lane = lax.broadcasted_iota(jnp.int32, q2.shape, 1)
    zero = jnp.zeros_like(q2)
    return jnp.concatenate([jnp.where(lane < NA_HEAD_DIM, q2, zero),
                            jnp.where(lane >= NA_HEAD_DIM, q2, zero)], axis=0)


def _merge_heads(o2):
    m = o2.shape[0] // 2
    lane = lax.broadcasted_iota(jnp.int32, (m, o2.shape[1]), 1)
    return jnp.where(lane < NA_HEAD_DIM, o2[:m], o2[m:])


def _na_kernel(q_ref, k_ref, v_ref, kc_ref, vc_ref, bias_ref, o_ref, *, rows):
    kh = NA_KH
    kc = kc_ref[0]
    vc = vc_ref[0]

    def row_body(r, carry):
        r0 = jnp.clip(r - kh // 2, 0, rows - kh)
        q_off = pl.multiple_of(r * GRID_W, GRID_W)
        k_off = pl.multiple_of(r0 * GRID_W, GRID_W)
        qs = _split_heads(q_ref[0, pl.ds(q_off, GRID_W), :])
        kspan = k_ref[0, pl.ds(k_off, kh * GRID_W), :]
        vspan = v_ref[0, pl.ds(k_off, kh * GRID_W), :]
        s_nb = lax.dot_general(qs, kspan, NT_DIMS, preferred_element_type=F32) + bias_ref[0, r - r0]
        s_cx = lax.dot_general(qs, kc, NT_DIMS, preferred_element_type=F32)
        m = jnp.maximum(jnp.max(s_nb, axis=-1, keepdims=True), jnp.max(s_cx, axis=-1, keepdims=True))
        p_nb = jnp.exp(s_nb - m)
        p_cx = jnp.exp(s_cx - m)
        denom = jnp.sum(p_nb, axis=-1, keepdims=True) + jnp.sum(p_cx, axis=-1, keepdims=True)
        o2 = (jnp.dot(p_nb.astype(BF16), vspan, preferred_element_type=F32)
              + jnp.dot(p_cx.astype(BF16), vc, preferred_element_type=F32)) / denom
        o_ref[0, pl.ds(q_off, GRID_W), :] = _merge_heads(o2).astype(BF16)
        return carry

    lax.fori_loop(0, rows, row_body, 0)


def _na_bias_tables(rpb):
    h = rpb.shape[0]
    cidx = jnp.arange(GRID_W)
    c0 = jnp.clip(cidx - NA_KW // 2, 0, GRID_W - NA_KW)
    in_win = (cidx[None, :] >= c0[:, None]) & (cidx[None, :] < c0[:, None] + NA_KW)
    dc = jnp.clip(cidx[None, :] - cidx[:, None], -(NA_KW - 1), NA_KW - 1) + (NA_KW - 1)
    dr = jnp.arange(NA_KH)[None, :] - jnp.arange(NA_KH)[:, None] + (NA_KH - 1)
    tab = rpb[:, dr[:, None, :, None], dc[None, :, None, :]]
    tab = jnp.where(in_win[None, None, :, None, :], tab.astype(F32), NEG_INF)
    tab = tab.reshape(h // 2, 2, NA_KH, GRID_W, NA_KH * GRID_W)
    return tab.transpose(0, 2, 1, 3, 4).reshape(h // 2, NA_KH, 2 * GRID_W, NA_KH * GRID_W)


def _na_attention(qkv, qkv_ctx, bias):
    b, t, _ = qkv.shape
    l = qkv_ctx.shape[1]
    rows = t // GRID_W
    assert rows >= NA_KH
    pairs = NA_HEADS // 2
    lanes = 2 * NA_HEAD_DIM
    return pl.pallas_call(
        functools.partial(_na_kernel, rows=rows),
        out_shape=jax.ShapeDtypeStruct((b, t, NA_WIDTH), BF16),
        grid=(b, pairs),
        in_specs=[pl.BlockSpec((1, t, lanes), lambda bi, hp: (bi, 0, hp)),
                  pl.BlockSpec((1, t, lanes), lambda bi, hp: (bi, 0, pairs + hp)),
                  pl.BlockSpec((1, t, lanes), lambda bi, hp: (bi, 0, 2 * pairs + hp)),
                  pl.BlockSpec((1, l, lanes), lambda bi, hp: (bi, 0, pairs + hp)),
                  pl.BlockSpec((1, l, lanes), lambda bi, hp: (bi, 0, 2 * pairs + hp)),
                  pl.BlockSpec((1, NA_KH, 2 * GRID_W, NA_KH * GRID_W), lambda bi, hp: (hp, 0, 0, 0))],
        out_specs=pl.BlockSpec((1, t, lanes), lambda bi, hp: (bi, 0, hp)),
        compiler_params=_params(2),
        name="na_attention",
    )(qkv, qkv, qkv, qkv_ctx, qkv_ctx, bias)


def _ctx_attn_kernel(q_ref, k_ref, v_ref, o_ref):
    qs = _split_heads(q_ref[0])
    s = lax.dot_general(qs, k_ref[0], NT_DIMS, preferred_element_type=F32)
    p = jnp.exp(s - jnp.max(s, axis=-1, keepdims=True))
    denom = jnp.sum(p, axis=-1, keepdims=True)
    o2 = jnp.dot(p.astype(BF16), v_ref[0], preferred_element_type=F32) / denom
    o_ref[0] = _merge_heads(o2).astype(BF16)


def _ctx_attention(qkv_ctx):
    b, l, _ = qkv_ctx.shape
    pairs = NA_HEADS // 2
    lanes = 2 * NA_HEAD_DIM
    return pl.pallas_call(
        _ctx_attn_kernel,
        out_shape=jax.ShapeDtypeStruct((b, l, NA_WIDTH), BF16),
        grid=(b, pairs),
        in_specs=[pl.BlockSpec((1, l, lanes), lambda bi, hp: (bi, 0, hp)),
                  pl.BlockSpec((1, l, lanes), lambda bi, hp: (bi, 0, pairs + hp)),
                  pl.BlockSpec((1, l, lanes), lambda bi, hp: (bi, 0, 2 * pairs + hp))],
        out_specs=pl.BlockSpec((1, l, lanes), lambda bi, hp: (bi, 0, hp)),
        compiler_params=_params(2),
        name="ctx_attention",
    )(qkv_ctx, qkv_ctx, qkv_ctx)


def _hyb_out_kernel(a_ref, u_ref, vn_ref, ws_ref, bs_ref, w_ref, x_ref, g1_ref, ng_ref, o_ref, s_ref):
    tm = a_ref.shape[1]
    c = SGU_CHUNK
    gd = vn_ref.shape[2] // SGU_GROUPS
    for ci in range(tm // c):
        rows = slice(ci * c, (ci + 1) * c)
        for gp in range(SGU_GROUPS // 2):
            lanes = slice(gp * 2 * gd, (gp + 1) * 2 * gd)
            mixed2 = jnp.dot(ws_ref[gp], vn_ref[0, rows, lanes], preferred_element_type=F32)
            lane = lax.broadcasted_iota(jnp.int32, (c, 2 * gd), 1)
            mixed = jnp.where(lane < gd, mixed2[:c], mixed2[c:]) + bs_ref[:, lanes]
            s_ref[rows, lanes] = (u_ref[0, rows, lanes].astype(F32) * mixed).astype(BF16)
    wa = a_ref.shape[2]
    y = (jnp.dot(a_ref[0], w_ref[:wa, :], preferred_element_type=F32)
         + jnp.dot(s_ref[...], w_ref[wa:, :], preferred_element_type=F32))
    o_ref[0] = x_ref[0] + g1_ref[0] * _rms(y, ng_ref[...])


def _hyb_out(a, u, vn, ws_pairs, bs_tab, w_bf, x, gate, gain):
    b, n, d = x.shape
    tm = _token_tile(n, 512)
    w = a.shape[2]
    row = lambda bi, i: (bi, 0, 0)
    tile = lambda bi, i: (bi, i, 0)
    return pl.pallas_call(
        _hyb_out_kernel,
        out_shape=jax.ShapeDtypeStruct((b, n, d), F32),
        grid=(b, n // tm),
        in_specs=[pl.BlockSpec((1, tm, w), tile),
                  pl.BlockSpec((1, tm, w), tile),
                  pl.BlockSpec((1, tm, w), tile),
                  _resident(ws_pairs.shape),
                  _resident(bs_tab.shape),
                  _resident(w_bf.shape),
                  pl.BlockSpec((1, tm, d), tile),
                  pl.BlockSpec((1, 1, d), row),
                  _resident((1, d))],
        out_specs=pl.BlockSpec((1, tm, d), tile),
        scratch_shapes=[pltpu.VMEM((tm, w), BF16)],
        compiler_params=_params(2),
        name="hyb_out",
    )(a, u, vn, ws_pairs, bs_tab, w_bf, x, gate, gain)


def _ret_in_kernel(x_ref, sh_ref, sc_ref, g_ref, w_ref, cos_ref, sin_ref, o_ref, *, rope, dk):
    hb = _norm_mod(x_ref[0], g_ref[...], sh_ref[0], sc_ref[0]).astype(BF16)
    qk_w = RET_HEADS * dk
    n_out = o_ref.shape[2]
    step = 512
    for c0 in range(0, n_out, step):
        acc = jnp.dot(hb, w_ref[:, c0:c0 + step], preferred_element_type=F32)
        if c0 < 2 * qk_w:
            if c0 >= qk_w:
                acc = acc * (dk ** -0.5)
            if rope:
                parts = []
                for hd in range(step // dk):
                    t = acc[:, hd * dk:(hd + 1) * dk]
                    parts.append(t * cos_ref[...] + pltpu.roll(t, dk // 2, 1) * sin_ref[...])
                acc = jnp.concatenate(parts, axis=-1)
        elif c0 >= 2 * qk_w + (n_out - 2 * qk_w) // 2:
            acc = _silu(acc)
        o_ref[0, :, c0:c0 + step] = acc.astype(BF16)


def _ret_in(x, shift, scale, gain, w_bf, cos2, sin2, rope):
    b, n, d = x.shape
    tm = _token_tile(n, 512)
    n_out = w_bf.shape[1]
    dk = cos2.shape[1]
    row = lambda bi, i: (bi, 0, 0)
    tile = lambda bi, i: (bi, i, 0)
    return pl.pallas_call(
        functools.partial(_ret_in_kernel, rope=rope, dk=dk),
        out_shape=jax.ShapeDtypeStruct((b, n, n_out), BF16),
        grid=(b, n // tm),
        in_specs=[pl.BlockSpec((1, tm, d), tile),
                  pl.BlockSpec((1, 1, d), row),
                  pl.BlockSpec((1, 1, d), row),
                  _resident((1, d)),
                  _resident(w_bf.shape),
                  pl.BlockSpec((tm, dk), lambda bi, i: (i, 0)),
                  pl.BlockSpec((tm, dk), lambda bi, i: (i, 0))],
        out_specs=pl.BlockSpec((1, tm, n_out), tile),
        compiler_params=_params(2),
        name="ret_in",
    )(x, shift, scale, gain, w_bf, cos2, sin2)


def _ret_scan_kernel(lg_ref, q_ref, k_ref, v_ref, s0f_ref, s0b_ref, on_ref, sf_ref, sb_ref, oacc_ref, st_ref):
    head = pl.program_id(1)
    n = q_ref.shape[1]
    c = RET_CHUNK
    nc = n // c
    pos = lax.broadcasted_iota(jnp.int32, (c, 1), 0).astype(F32)
    diff = (lax.broadcasted_iota(jnp.int32, (c, c), 0) - lax.broadcasted_iota(jnp.int32, (c, c), 1)).astype(F32)

    for direction in range(2):
        lg = lg_ref[direction, head]
        if direction == 0:
            dist = diff
            dq = jnp.exp(lg * (pos + 1.0))
            dkk = jnp.exp(lg * (c - 1.0 - pos))
            st_ref[...] = s0f_ref[0, 0]
        else:
            dist = -diff
            dq = jnp.exp(lg * (c - pos))
            dkk = jnp.exp(lg * pos)
            st_ref[...] = s0b_ref[0, 0]
        inside = dist >= 0
        decay = jnp.where(inside, jnp.exp(lg * jnp.where(inside, dist, 0.0)), 0.0)
        dchunk = jnp.exp(lg * c)

        def chunk_body(i, carry, direction=direction, decay=decay, dq=dq, dkk=dkk, dchunk=dchunk):
            ci = i if direction == 0 else nc - 1 - i
            off = pl.multiple_of(ci * c, c)
            q = q_ref[0, pl.ds(off, c), :]
            k = k_ref[0, pl.ds(off, c), :]
            v = v_ref[0, pl.ds(off, c), :]
            state = st_ref[...]
            s = lax.dot_general(q, k, NT_DIMS, preferred_element_type=F32) * decay
            o = (jnp.dot(s.astype(BF16), v, preferred_element_type=F32)
                 + dq * jnp.dot(q, state.astype(BF16), preferred_element_type=F32))
            kd = (k.astype(F32) * dkk).astype(BF16)
            st_ref[...] = state * dchunk + lax.dot_general(kd, v, TN_DIMS, preferred_element_type=F32)
            if direction == 0:
                oacc_ref[pl.ds(off, c), :] = o
            else:
                tot = oacc_ref[pl.ds(off, c), :] + o
                mu = jnp.mean(tot, axis=-1, keepdims=True)
                tc = tot - mu
                var = jnp.mean(tc * tc, axis=-1, keepdims=True)
                on_ref[0, pl.ds(off, c), :] = (tc * lax.rsqrt(var + EPS)).astype(BF16)
            return carry

        lax.fori_loop(0, nc, chunk_body, 0)
        if direction == 0:
            sf_ref[0, 0] = st_ref[...]
        else:
            sb_ref[0, 0] = st_ref[...]


def _ret_scan(qkvg, log_decay, s0f, s0b, dk, dv):
    b, n, _ = qkvg.shape
    h = RET_HEADS
    v_blk0 = (2 * h * dk) // dv
    state = jax.ShapeDtypeStruct((b, h, dk, dv), F32)
    st_spec = pl.BlockSpec((1, 1, dk, dv), lambda bi, hi: (bi, hi, 0, 0))
    return pl.pallas_call(
        _ret_scan_kernel,
        out_shape=(jax.ShapeDtypeStruct((b, n, h * dv), BF16), state, state),
        grid=(b, h),
        in_specs=[pl.BlockSpec(memory_space=pltpu.SMEM),
                  pl.BlockSpec((1, n, dk), lambda bi, hi: (bi, 0, hi)),
                  pl.BlockSpec((1, n, dk), lambda bi, hi: (bi, 0, h + hi)),
                  pl.BlockSpec((1, n, dv), lambda bi, hi: (bi, 0, v_blk0 + hi)),
                  st_spec, st_spec],
        out_specs=(pl.BlockSpec((1, n, dv), lambda bi, hi: (bi, 0, hi)), st_spec, st_spec),
        scratch_shapes=[pltpu.VMEM((n, dv), F32), pltpu.VMEM((dk, dv), F32)],
        compiler_params=_params(2),
        name="ret_scan",
    )(log_decay, qkvg, qkvg, qkvg, s0f, s0b)


def _ret_out_kernel(gt_ref, on_ref, gn_ref, w_ref, x_ref, g1_ref, ng_ref, o_ref):
    z = (gt_ref[0].astype(F32) * (on_ref[0].astype(F32) * gn_ref[...])).astype(BF16)
    y = jnp.dot(z, w_ref[...], preferred_element_type=F32)
    o_ref[0] = x_ref[0] + g1_ref[0] * _rms(y, ng_ref[...])


def _ret_out(qkvg, on, gn_g, w_bf, x, gate, gain):
    b, n, d = x.shape
    tm = _token_tile(n, 512)
    vw = on.shape[2]
    g_blk = qkvg.shape[2] // vw - 1
    row = lambda bi, i: (bi, 0, 0)
    tile = lambda bi, i: (bi, i, 0)
    return pl.pallas_call(
        _ret_out_kernel,
        out_shape=jax.ShapeDtypeStruct((b, n, d), F32),
        grid=(b, n // tm),
        in_specs=[pl.BlockSpec((1, tm, vw), lambda bi, i: (bi, i, g_blk)),
                  pl.BlockSpec((1, tm, vw), tile),
                  _resident((1, vw)),
                  _resident(w_bf.shape),
                  pl.BlockSpec((1, tm, d), tile),
                  pl.BlockSpec((1, 1, d), row),
                  _resident((1, d))],
        out_specs=pl.BlockSpec((1, tm, d), tile),
        compiler_params=_params(2),
        name="ret_out",
    )(qkvg, on, gn_g, w_bf, x, gate, gain)


def _ffn_kernel(xp_ref, x_ref, xn_ref, sh_ref, sc_ref, g_ref, win_ref, cw_ref, cb_ref, wout_ref, g2_ref, ng_ref,
                o_ref, h_ref, a_ref, acc_ref):
    i = pl.program_id(1)
    last = pl.num_programs(1) - 1
    tm = x_ref.shape[1]
    gain, shift, scale = g_ref[...], sh_ref[0], sc_ref[0]
    hp = jnp.where(i > 0, _norm_mod(xp_ref[0], gain, shift, scale), 0.0)
    hn = jnp.where(i < last, _norm_mod(xn_ref[0], gain, shift, scale), 0.0)
    h_ref[:HALO] = hp.astype(BF16)
    h_ref[HALO:HALO + tm] = _norm_mod(x_ref[0], gain, shift, scale).astype(BF16)
    h_ref[HALO + tm:] = hn.astype(BF16)
    fc = wout_ref.shape[1]

    def chunk_body(ci, carry):
        a_ref[...] = jnp.dot(h_ref[...], win_ref[ci], preferred_element_type=F32)
        cw = cw_ref[ci]
        conv = (a_ref[pl.ds(HALO - 1, tm), :] * cw[0:1] + a_ref[pl.ds(HALO, tm), :] * cw[1:2]
                + a_ref[pl.ds(HALO + 1, tm), :] * cw[2:3] + cb_ref[ci])
        act = (_silu(conv[:, :fc]) * conv[:, fc:]).astype(BF16)
        part = jnp.dot(act, wout_ref[ci], preferred_element_type=F32)

        @pl.when(ci == 0)
        def _():
            acc_ref[...] = part

        @pl.when(ci > 0)
        def _():
            acc_ref[...] += part

        return carry

    lax.fori_loop(0, win_ref.shape[0], chunk_body, 0)
    o_ref[0] = x_ref[0] + g2_ref[0] * _rms(acc_ref[...], ng_ref[...])


def _conv_ffn(x, shift, scale, gain, win_c, cw_c, cb_c, wout_c, gate, gain_out):
    b, n, d = x.shape
    tm = _token_tile(n, 512)
    assert tm % HALO == 0
    hb = tm // HALO
    n_hb = n // HALO
    row = lambda bi, i: (bi, 0, 0)
    tile = lambda bi, i: (bi, i, 0)
    return pl.pallas_call(
        _ffn_kernel,
        out_shape=jax.ShapeDtypeStruct((b, n, d), F32),
        grid=(b, n // tm),
        in_specs=[pl.BlockSpec((1, HALO, d), lambda bi, i: (bi, jnp.maximum(i * hb - 1, 0), 0)),
                  pl.BlockSpec((1, tm, d), tile),
                  pl.BlockSpec((1, HALO, d), lambda bi, i: (bi, jnp.minimum((i + 1) * hb, n_hb - 1), 0)),
                  pl.BlockSpec((1, 1, d), row),
                  pl.BlockSpec((1, 1, d), row),
                  _resident((1, d)),
                  _resident(win_c.shape),
                  _resident(cw_c.shape),
                  _resident(cb_c.shape),
                  _resident(wout_c.shape),
                  pl.BlockSpec((1, 1, d), row),
                  _resident((1, d))],
        out_specs=pl.BlockSpec((1, tm, d), tile),
        scratch_shapes=[pltpu.VMEM((tm + 2 * HALO, d), BF16),
                        pltpu.VMEM((tm + 2 * HALO, win_c.shape[2]), F32),
                        pltpu.VMEM((tm, d), F32)],
        compiler_params=_params(2),
        name="conv_ffn",
    )(x, x, x, shift, scale, gain, win_c, cw_c, cb_c, wout_c, gate, gain_out)


def _ffn_weights(w_in, conv_w, conv_b, w_out):
    d, two_ff = w_in.shape
    ff = two_ff // 2
    nch = ff // FFN_CHUNK
    assert nch * FFN_CHUNK == ff
    pair = lambda a: jnp.concatenate([a[..., :ff].reshape(a.shape[:-1] + (nch, FFN_CHUNK)),
                                      a[..., ff:].reshape(a.shape[:-1] + (nch, FFN_CHUNK))], axis=-1)
    win_c = pair(w_in).transpose(1, 0, 2).astype(BF16)
    cw_c = pair(conv_w).transpose(1, 0, 2)
    cb_c = pair(conv_b)[:, None, :]
    wout_c = w_out.reshape(nch, FFN_CHUNK, w_out.shape[1]).astype(BF16)
    return win_c, cw_c, cb_c, wout_c


def _rope_tables(t, dk):
    tok = jnp.arange(t)
    row = (tok // GRID_W).astype(F32)
    col = (tok % GRID_W).astype(F32)
    n_freq = dk // 4
    inv = ROPE_BASE ** (-jnp.arange(n_freq, dtype=F32) / n_freq)
    ang = jnp.concatenate([row[:, None] * inv, col[:, None] * inv], axis=-1)
    cos, sin = jnp.cos(ang), jnp.sin(ang)
    return jnp.concatenate([cos, cos], axis=-1), jnp.concatenate([-sin, sin], axis=-1)


def kernel(x, c, ctx, c_ctx, ada_w, ada_b, norm_g, hyb_w_in, na_rpb, sgu_w, sgu_b, hyb_w_out, ret_w_in, ret_log_decay,
           ret_gn_g, ret_w_out, ffn_w_in, ffn_conv_w, ffn_conv_b, ffn_w_out):
    b, t, d = x.shape
    depth = ada_w.shape[0]
    dk = d // RET_HEADS
    dv = 2 * dk

    cond = jnp.zeros((8, d), F32).at[:b].set(c).at[b].set(c_ctx)
    mods = _ada_rows(cond, ada_w, ada_b)
    cos2, sin2 = _rope_tables(t, dk)
    xc = ctx

    for i in range(depth):
        ctx_out = i < depth - 1
        j = i // 2
        lat = [mods[i, :b, m * d:(m + 1) * d][:, None, :] for m in range(6)]
        cxm = [jnp.broadcast_to(mods[i, b, m * d:(m + 1) * d][None, None, :], (b, 1, d)) for m in range(6)]
        gains = [norm_g[i, m][None, :] for m in range(4)]

        if i % 2 == 0:
            w_in = hyb_w_in[j].astype(BF16)
            w_out = hyb_w_out[j].astype(BF16)
            ws_pairs = sgu_w[j].astype(BF16).reshape(SGU_GROUPS // 2, 2 * SGU_CHUNK, SGU_CHUNK)
            bs_tab = jnp.repeat(sgu_b[j].T, (d - NA_WIDTH) // SGU_GROUPS, axis=1)
            bias = _na_bias_tables(na_rpb[j])
            qkv_c, u_c, vn_c = _hyb_in(xc, cxm[0], cxm[1], gains[0], w_in)
            qkv_l, u_l, vn_l = _hyb_in(x, lat[0], lat[1], gains[0], w_in)
            a_l = _na_attention(qkv_l, qkv_c, bias)
            x = _hyb_out(a_l, u_l, vn_l, ws_pairs, bs_tab, w_out, x, lat[2], gains[1])
            if ctx_out:
                a_c = _ctx_attention(qkv_c)
                xc = _hyb_out(a_c, u_c, vn_c, ws_pairs, bs_tab, w_out, xc, cxm[2], gains[1])
        else:
            w_in = ret_w_in[j].astype(BF16)
            w_out = ret_w_out[j].astype(BF16)
            lg = ret_log_decay[j].astype(F32)
            gn = ret_gn_g[j][None, :]
            zero_state = jnp.zeros((b, RET_HEADS, dk, dv), F32)
            p_c = _ret_in(xc, cxm[0], cxm[1], gains[0], w_in, cos2, sin2, rope=False)
            p_l = _ret_in(x, lat[0], lat[1], gains[0], w_in, cos2, sin2, rope=True)
            on_c, sf, sb = _ret_scan(p_c, lg, zero_state, zero_state, dk, dv)
            on_l, _, _ = _ret_scan(p_l, lg, sf, sb, dk, dv)
            x = _ret_out(p_l, on_l, gn, w_out, x, lat[2], gains[1])
            if ctx_out:
                xc = _ret_out(p_c, on_c, gn, w_out, xc, cxm[2], gains[1])

        ffn_w = _ffn_weights(ffn_w_in[i], ffn_conv_w[i], ffn_conv_b[i], ffn_w_out[i])
        x = _conv_ffn(x, lat[3], lat[4], gains[2], *ffn_w, lat[5], gains[3])
        if ctx_out:
            xc = _conv_ffn(xc, cxm[3], cxm[4], gains[2], *ffn_w, cxm[5], gains[3])
    return x
```

```python
import functools

import jax
import jax.numpy as jnp
from jax import lax
from jax.experimental import pallas as pl
from jax.experimental.pallas import tpu as pltpu

F32 = jnp.float32
BF16 = jnp.bfloat16

GRID_W = 64
EPS = 1e-6
NEG_INF = -1e30

NA_HEADS = 8
NA_HEAD_DIM = 64
NA_WIDTH = NA_HEADS * NA_HEAD_DIM
NA_KH = 8
NA_KW = 16
SGU_GROUPS = 8
SGU_CHUNK = 128

RET_HEADS = 8
RET_CHUNK = 128
ROPE_BASE = 10000.0

CONV_W = 3
FFN_CHUNK = 256
HALO = 16

V7X_VMEM_LIMIT_BYTES = 56 * 1024 * 1024

NT_DIMS = (((1,), (1,)), ((), ()))
TN_DIMS = (((0,), (0,)), ((), ()))


def _params(n_axes):
    return pltpu.CompilerParams(dimension_semantics=("arbitrary",) * n_axes,
                                vmem_limit_bytes=V7X_VMEM_LIMIT_BYTES)


def _resident(shape):
    zeros = (0,) * len(shape)
    return pl.BlockSpec(shape, lambda *_: zeros, pipeline_mode=pl.Buffered(1))


def _token_tile(n, want):
    tm = min(n, want)
    assert n % tm == 0
    return tm


def _rms(v, gain):
    return v * lax.rsqrt(jnp.mean(v * v, axis=-1, keepdims=True) + EPS) * gain


def _silu(v):
    return v * (1.0 / (1.0 + jnp.exp(-v)))


def _norm_mod(x, gain, shift, scale):
    return _rms(x, gain) * (1.0 + scale) + shift


def _ada_kernel(c_ref, w_ref, b_ref, o_ref):
    s = _silu(c_ref[...])
    o_ref[0] = jnp.dot(s, w_ref[0], preferred_element_type=F32) + b_ref[0]


def _ada_rows(cond, ada_w, ada_b):
    depth, d, n = ada_w.shape
    tn = 2048
    return pl.pallas_call(
        _ada_kernel,
        out_shape=jax.ShapeDtypeStruct((depth, 8, n), F32),
        grid=(depth, n // tn),
        in_specs=[pl.BlockSpec((8, d), lambda i, j: (0, 0)),
                  pl.BlockSpec((1, d, tn), lambda i, j: (i, 0, j)),
                  pl.BlockSpec((1, 1, tn), lambda i, j: (i, 0, j))],
        out_specs=pl.BlockSpec((1, 8, tn), lambda i, j: (i, 0, j)),
        compiler_params=_params(2),
        name="ada_rows",
    )(cond, ada_w, ada_b.reshape(depth, 1, n))


def _hyb_in_kernel(x_ref, sh_ref, sc_ref, g_ref, w_ref, qkv_ref, u_ref, vn_ref):
    hb = _norm_mod(x_ref[0], g_ref[...], sh_ref[0], sc_ref[0]).astype(BF16)
    w = NA_WIDTH
    for part in range(3):
        acc = jnp.dot(hb, w_ref[:, part * w:(part + 1) * w], preferred_element_type=F32)
        if part == 0:
            acc = acc * (NA_HEAD_DIM ** -0.5)
        qkv_ref[0, :, part * w:(part + 1) * w] = acc.astype(BF16)
    u = jax.nn.gelu(jnp.dot(hb, w_ref[:, 3 * w:4 * w], preferred_element_type=F32))
    u_ref[0] = u.astype(BF16)
    g = jax.nn.gelu(jnp.dot(hb, w_ref[:, 4 * w:5 * w], preferred_element_type=F32))
    mu = jnp.mean(g, axis=-1, keepdims=True)
    gc = g - mu
    var = jnp.mean(gc * gc, axis=-1, keepdims=True)
    vn_ref[0] = (gc * lax.rsqrt(var + EPS)).astype(BF16)


def _hyb_in(x, shift, scale, gain, w_bf):
    b, n, d = x.shape
    tm = _token_tile(n, 512)
    w = NA_WIDTH
    row = lambda bi, i: (bi, 0, 0)
    tile = lambda bi, i: (bi, i, 0)
    return pl.pallas_call(
        _hyb_in_kernel,
        out_shape=(jax.ShapeDtypeStruct((b, n, 3 * w), BF16),
                   jax.ShapeDtypeStruct((b, n, w), BF16),
                   jax.ShapeDtypeStruct((b, n, w), BF16)),
        grid=(b, n // tm),
        in_specs=[pl.BlockSpec((1, tm, d), tile),
                  pl.BlockSpec((1, 1, d), row),
                  pl.BlockSpec((1, 1, d), row),
                  _resident((1, d)),
                  _resident(w_bf.shape)],
        out_specs=(pl.BlockSpec((1, tm, 3 * w), tile),
                   pl.BlockSpec((1, tm, w), tile),
                   pl.BlockSpec((1, tm, w), tile)),
        compiler_params=_params(2),
        name="hyb_in",
    )(x, shift, scale, gain, w_bf)


def _split_heads(q2):
    lane = lax.broadcasted_iota(jnp.int32, q2.shape, 1)
    zero = jnp.zeros_like(q2)
    return jnp.concatenate([jnp.where(lane < NA_HEAD_DIM, q2, zero),
                            jnp.where(lane >= NA_HEAD_DIM, q2, zero)], axis=0)


def _merge_heads(o2):
    m = o2.shape[0] // 2
    lane = lax.broadcasted_iota(jnp.int32, (m, o2.shape[1]), 1)
    return jnp.where(lane < NA_HEAD_DIM, o2[:m], o2[m:])


def _na_kernel(q_ref, k_ref, v_ref, kc_ref, vc_ref, bias_ref, o_ref, *, rows):
    kh = NA_KH
    kc = kc_ref[0]
    vc = vc_ref[0]

    def row_body(r, carry):
        r0 = jnp.clip(r - kh // 2, 0, rows - kh)
        q_off = pl.multiple_of(r * GRID_W, GRID_W)
        k_off = pl.multiple_of(r0 * GRID_W, GRID_W)
        qs = _split_heads(q_ref[0, pl.ds(q_off, GRID_W), :])
        kspan = k_ref[0, pl.ds(k_off, kh * GRID_W), :]
        vspan = v_ref[0, pl.ds(k_off, kh * GRID_W), :]
        s_nb = lax.dot_general(qs, kspan, NT_DIMS, preferred_element_type=F32) + bias_ref[0, r - r0]
        s_cx = lax.dot_general(qs, kc, NT_DIMS, preferred_element_type=F32)
        m = jnp.maximum(jnp.max(s_nb, axis=-1, keepdims=True), jnp.max(s_cx, axis=-1, keepdims=True))
        p_nb = jnp.exp(s_nb - m)
        p_cx = jnp.exp(s_cx - m)
        denom = jnp.sum(p_nb, axis=-1, keepdims=True) + jnp.sum(p_cx, axis=-1, keepdims=True)
        o2 = (jnp.dot(p_nb.astype(BF16), vspan, preferred_element_type=F32)
              + jnp.dot(p_cx.astype(BF16), vc, preferred_element_type=F32)) / denom
        o_ref[0, pl.ds(q_off, GRID_W), :] = _merge_heads(o2).astype(BF16)
        return carry

    lax.fori_loop(0, rows, row_body, 0, unroll=2)


def _na_bias_tables(rpb):
    h = rpb.shape[0]
    w, kw = GRID_W, NA_KW
    cidx = jnp.arange(w)
    c0 = jnp.clip(cidx - kw // 2, 0, w - kw)
    in_win = (cidx[None, :] >= c0[:, None]) & (cidx[None, :] < c0[:, None] + kw)
    edge = w - kw
    ext = jnp.concatenate([jnp.broadcast_to(rpb[..., :1], rpb.shape[:-1] + (edge,)), rpb.astype(F32),
                           jnp.broadcast_to(rpb[..., -1:], rpb.shape[:-1] + (edge,))], axis=-1)
    toep = jnp.stack([ext[..., w - 1 - q:2 * w - 1 - q] for q in range(w)], axis=-2)
    toep = jnp.where(in_win, toep, NEG_INF)
    tab = jnp.stack([toep[:, NA_KH - 1 - o:2 * NA_KH - 1 - o] for o in range(NA_KH)], axis=1)
    tab = tab.reshape(h // 2, 2, NA_KH, NA_KH, w, w).transpose(0, 2, 1, 4, 3, 5)
    return tab.reshape(h // 2, NA_KH, 2 * w, NA_KH * w)


def _na_attention(qkv, qkv_ctx, bias):
    b, t, _ = qkv.shape
    l = qkv_ctx.shape[1]
    rows = t // GRID_W
    assert rows >= NA_KH
    pairs = NA_HEADS // 2
    lanes = 2 * NA_HEAD_DIM
    return pl.pallas_call(
        functools.partial(_na_kernel, rows=rows),
        out_shape=jax.ShapeDtypeStruct((b, t, NA_WIDTH), BF16),
        grid=(b, pairs),
        in_specs=[pl.BlockSpec((1, t, lanes), lambda bi, hp: (bi, 0, hp)),
                  pl.BlockSpec((1, t, lanes), lambda bi, hp: (bi, 0, pairs + hp)),
                  pl.BlockSpec((1, t, lanes), lambda bi, hp: (bi, 0, 2 * pairs + hp)),
                  pl.BlockSpec((1, l, lanes), lambda bi, hp: (bi, 0, pairs + hp)),
                  pl.BlockSpec((1, l, lanes), lambda bi, hp: (bi, 0, 2 * pairs + hp)),
                  pl.BlockSpec((1, NA_KH, 2 * GRID_W, NA_KH * GRID_W), lambda bi, hp: (hp, 0, 0, 0))],
        out_specs=pl.BlockSpec((1, t, lanes), lambda bi, hp: (bi, 0, hp)),
        compiler_params=_params(2),
        name="na_attention",
    )(qkv, qkv, qkv, qkv_ctx, qkv_ctx, bias)


def _ctx_attn_kernel(q_ref, k_ref, v_ref, o_ref):
    qs = _split_heads(q_ref[0])
    s = lax.dot_general(qs, k_ref[0], NT_DIMS, preferred_element_type=F32)
    p = jnp.exp(s - jnp.max(s, axis=-1, keepdims=True))
    denom = jnp.sum(p, axis=-1, keepdims=True)
    o2 = jnp.dot(p.astype(BF16), v_ref[0], preferred_element_type=F32) / denom
    o_ref[0] = _merge_heads(o2).astype(BF16)


def _ctx_attention(qkv_ctx):
    b, l, _ = qkv_ctx.shape
    pairs = NA_HEADS // 2
    lanes = 2 * NA_HEAD_DIM
    return pl.pallas_call(
        _ctx_attn_kernel,
        out_shape=jax.ShapeDtypeStruct((b, l, NA_WIDTH), BF16),
        grid=(b, pairs),
        in_specs=[pl.BlockSpec((1, l, lanes), lambda bi, hp: (bi, 0, hp)),
                  pl.BlockSpec((1, l, lanes), lambda bi, hp: (bi, 0, pairs + hp)),
                  pl.BlockSpec((1, l, lanes), lambda bi, hp: (bi, 0, 2 * pairs + hp))],
        out_specs=pl.BlockSpec((1, l, lanes), lambda bi, hp: (bi, 0, hp)),
        compiler_params=_params(2),
        name="ctx_attention",
    )(qkv_ctx, qkv_ctx, qkv_ctx)


def _hyb_out_kernel(a_ref, u_ref, vn_ref, ws_ref, bs_ref, w_ref, x_ref, g1_ref, ng_ref, o_ref, s_ref):
    tm = a_ref.shape[1]
    c = SGU_CHUNK
    gd = vn_ref.shape[2] // SGU_GROUPS
    for ci in range(tm // c):
        rows = slice(ci * c, (ci + 1) * c)
        for gp in range(SGU_GROUPS // 2):
            lanes = slice(gp * 2 * gd, (gp + 1) * 2 * gd)
            mixed2 = jnp.dot(ws_ref[gp], vn_ref[0, rows, lanes], preferred_element_type=F32)
            lane = lax.broadcasted_iota(jnp.int32, (c, 2 * gd), 1)
            mixed = jnp.where(lane < gd, mixed2[:c], mixed2[c:]) + bs_ref[:, lanes]
            s_ref[rows, lanes] = (u_ref[0, rows, lanes].astype(F32) * mixed).astype(BF16)
    wa = a_ref.shape[2]
    y = (jnp.dot(a_ref[0], w_ref[:wa, :], preferred_element_type=F32)
         + jnp.dot(s_ref[...], w_ref[wa:, :], preferred_element_type=F32))
    o_ref[0] = x_ref[0] + g1_ref[0] * _rms(y, ng_ref[...])


def _hyb_out(a, u, vn, ws_pairs, bs_tab, w_bf, x, gate, gain):
    b, n, d = x.shape
    tm = _token_tile(n, 512)
    w = a.shape[2]
    row = lambda bi, i: (bi, 0, 0)
    tile = lambda bi, i: (bi, i, 0)
    return pl.pallas_call(
        _hyb_out_kernel,
        out_shape=jax.ShapeDtypeStruct((b, n, d), F32),
        grid=(b, n // tm),
        in_specs=[pl.BlockSpec((1, tm, w), tile),
                  pl.BlockSpec((1, tm, w), tile),
                  pl.BlockSpec((1, tm, w), tile),
                  _resident(ws_pairs.shape),
                  _resident(bs_tab.shape),
                  _resident(w_bf.shape),
                  pl.BlockSpec((1, tm, d), tile),
                  pl.BlockSpec((1, 1, d), row),
                  _resident((1, d))],
        out_specs=pl.BlockSpec((1, tm, d), tile),
        scratch_shapes=[pltpu.VMEM((tm, w), BF16)],
        compiler_params=_params(2),
        name="hyb_out",
    )(a, u, vn, ws_pairs, bs_tab, w_bf, x, gate, gain)


def _ret_in_kernel(x_ref, sh_ref, sc_ref, g_ref, w_ref, cos_ref, sin_ref, o_ref, *, rope, dk):
    hb = _norm_mod(x_ref[0], g_ref[...], sh_ref[0], sc_ref[0]).astype(BF16)
    qk_w = RET_HEADS * dk
    n_out = o_ref.shape[2]
    step = 512
    for c0 in range(0, n_out, step):
        acc = jnp.dot(hb, w_ref[:, c0:c0 + step], preferred_element_type=F32)
        if c0 < 2 * qk_w:
            if c0 >= qk_w:
                acc = acc * (dk ** -0.5)
            if rope:
                parts = []
                for hd in range(step // dk):
                    t = acc[:, hd * dk:(hd + 1) * dk]
                    parts.append(t * cos_ref[...] + pltpu.roll(t, dk // 2, 1) * sin_ref[...])
                acc = jnp.concatenate(parts, axis=-1)
        elif c0 >= 2 * qk_w + (n_out - 2 * qk_w) // 2:
            acc = _silu(acc)
        o_ref[0, :, c0:c0 + step] = acc.astype(BF16)


def _ret_in(x, shift, scale, gain, w_bf, cos2, sin2, rope):
    b, n, d = x.shape
    tm = _token_tile(n, 512)
    n_out = w_bf.shape[1]
    dk = cos2.shape[1]
    row = lambda bi, i: (bi, 0, 0)
    tile = lambda bi, i: (bi, i, 0)
    return pl.pallas_call(
        functools.partial(_ret_in_kernel, rope=rope, dk=dk),
        out_shape=jax.ShapeDtypeStruct((b, n, n_out), BF16),
        grid=(b, n // tm),
        in_specs=[pl.BlockSpec((1, tm, d), tile),
                  pl.BlockSpec((1, 1, d), row),
                  pl.BlockSpec((1, 1, d), row),
                  _resident((1, d)),
                  _resident(w_bf.shape),
                  pl.BlockSpec((tm, dk), lambda bi, i: (i, 0)),
                  pl.BlockSpec((tm, dk), lambda bi, i: (i, 0))],
        out_specs=pl.BlockSpec((1, tm, n_out), tile),
        compiler_params=_params(2),
        name="ret_in",
    )(x, shift, scale, gain, w_bf, cos2, sin2)


def _ret_scan_kernel(lg_ref, q_ref, k_ref, v_ref, s0f_ref, s0b_ref, on_ref, sf_ref, sb_ref, oacc_ref, st_ref):
    head = pl.program_id(1)
    n = q_ref.shape[1]
    c = RET_CHUNK
    nc = n // c
    pos = lax.broadcasted_iota(jnp.int32, (c, 1), 0).astype(F32)
    diff = (lax.broadcasted_iota(jnp.int32, (c, c), 0) - lax.broadcasted_iota(jnp.int32, (c, c), 1)).astype(F32)

    for direction in range(2):
        lg = lg_ref[direction, head]
        if direction == 0:
            dist = diff
            dq = jnp.exp(lg * (pos + 1.0))
            dkk = jnp.exp(lg * (c - 1.0 - pos))
            st_ref[...] = s0f_ref[0, 0]
        else:
            dist = -diff
            dq = jnp.exp(lg * (c - pos))
            dkk = jnp.exp(lg * pos)
            st_ref[...] = s0b_ref[0, 0]
        inside = dist >= 0
        decay = jnp.where(inside, jnp.exp(lg * jnp.where(inside, dist, 0.0)), 0.0)
        dchunk = jnp.exp(lg * c)

        def chunk_body(i, carry, direction=direction, decay=decay, dq=dq, dkk=dkk, dchunk=dchunk):
            ci = i if direction == 0 else nc - 1 - i
            off = pl.multiple_of(ci * c, c)
            q = q_ref[0, pl.ds(off, c), :]
            k = k_ref[0, pl.ds(off, c), :]
            v = v_ref[0, pl.ds(off, c), :]
            state = st_ref[...]
            s = lax.dot_general(q, k, NT_DIMS, preferred_element_type=F32) * decay
            o = (jnp.dot(s.astype(BF16), v, preferred_element_type=F32)
                 + dq * jnp.dot(q, state.astype(BF16), preferred_element_type=F32))
            kd = (k.astype(F32) * dkk).astype(BF16)
            st_ref[...] = state * dchunk + lax.dot_general(kd, v, TN_DIMS, preferred_element_type=F32)
            if direction == 0:
                oacc_ref[pl.ds(off, c), :] = o
            else:
                tot = oacc_ref[pl.ds(off, c), :] + o
                mu = jnp.mean(tot, axis=-1, keepdims=True)
                tc = tot - mu
                var = jnp.mean(tc * tc, axis=-1, keepdims=True)
                on_ref[0, pl.ds(off, c), :] = (tc * lax.rsqrt(var + EPS)).astype(BF16)
            return carry

        lax.fori_loop(0, nc, chunk_body, 0)
        if direction == 0:
            sf_ref[0, 0] = st_ref[...]
        else:
            sb_ref[0, 0] = st_ref[...]


def _ret_scan(qkvg, log_decay, s0f, s0b, dk, dv):
    b, n, _ = qkvg.shape
    h = RET_HEADS
    v_blk0 = (2 * h * dk) // dv
    state = jax.ShapeDtypeStruct((b, h, dk, dv), F32)
    st_spec = pl.BlockSpec((1, 1, dk, dv), lambda bi, hi: (bi, hi, 0, 0))
    return pl.pallas_call(
        _ret_scan_kernel,
        out_shape=(jax.ShapeDtypeStruct((b, n, h * dv), BF16), state, state),
        grid=(b, h),
        in_specs=[pl.BlockSpec(memory_space=pltpu.SMEM),
                  pl.BlockSpec((1, n, dk), lambda bi, hi: (bi, 0, hi)),
                  pl.BlockSpec((1, n, dk), lambda bi, hi: (bi, 0, h + hi)),
                  pl.BlockSpec((1, n, dv), lambda bi, hi: (bi, 0, v_blk0 + hi)),
                  st_spec, st_spec],
        out_specs=(pl.BlockSpec((1, n, dv), lambda bi, hi: (bi, 0, hi)), st_spec, st_spec),
        scratch_shapes=[pltpu.VMEM((n, dv), F32), pltpu.VMEM((dk, dv), F32)],
        compiler_params=_params(2),
        name="ret_scan",
    )(log_decay, qkvg, qkvg, qkvg, s0f, s0b)


def _ret_out_kernel(gt_ref, on_ref, gn_ref, w_ref, x_ref, g1_ref, ng_ref, o_ref):
    z = (gt_ref[0].astype(F32) * (on_ref[0].astype(F32) * gn_ref[...])).astype(BF16)
    y = jnp.dot(z, w_ref[...], preferred_element_type=F32)
    o_ref[0] = x_ref[0] + g1_ref[0] * _rms(y, ng_ref[...])


def _ret_out(qkvg, on, gn_g, w_bf, x, gate, gain):
    b, n, d = x.shape
    tm = _token_tile(n, 512)
    vw = on.shape[2]
    g_blk = qkvg.shape[2] // vw - 1
    row = lambda bi, i: (bi, 0, 0)
    tile = lambda bi, i: (bi, i, 0)
    return pl.pallas_call(
        _ret_out_kernel,
        out_shape=jax.ShapeDtypeStruct((b, n, d), F32),
        grid=(b, n // tm),
        in_specs=[pl.BlockSpec((1, tm, vw), lambda bi, i: (bi, i, g_blk)),
                  pl.BlockSpec((1, tm, vw), tile),
                  _resident((1, vw)),
                  _resident(w_bf.shape),
                  pl.BlockSpec((1, tm, d), tile),
                  pl.BlockSpec((1, 1, d), row),
                  _resident((1, d))],
        out_specs=pl.BlockSpec((1, tm, d), tile),
        compiler_params=_params(2),
        name="ret_out",
    )(qkvg, on, gn_g, w_bf, x, gate, gain)


def _ffn_kernel(xp_ref, x_ref, xn_ref, sh_ref, sc_ref, g_ref, win_ref, cw_ref, cb_ref, wout_ref, g2_ref, ng_ref,
                o_ref, h_ref, a0_ref, a1_ref, acc_ref):
    i = pl.program_id(1)
    last = pl.num_programs(1) - 1
    tm = x_ref.shape[1]
    gain, shift, scale = g_ref[...], sh_ref[0], sc_ref[0]
    hp = jnp.where(i > 0, _norm_mod(xp_ref[0], gain, shift, scale), 0.0)
    hn = jnp.where(i < last, _norm_mod(xn_ref[0], gain, shift, scale), 0.0)
    h_ref[:HALO] = hp.astype(BF16)
    h_ref[HALO:HALO + tm] = _norm_mod(x_ref[0], gain, shift, scale).astype(BF16)
    h_ref[HALO + tm:] = hn.astype(BF16)
    fc = wout_ref.shape[1]

    nch = win_ref.shape[0]

    def project(ci, a_ref):
        a_ref[...] = jnp.dot(h_ref[...], win_ref[ci], preferred_element_type=F32)

    def mix(ci, a_ref):
        cw = cw_ref[ci]
        conv = (a_ref[pl.ds(HALO - 1, tm), :] * cw[0:1] + a_ref[pl.ds(HALO, tm), :] * cw[1:2]
                + a_ref[pl.ds(HALO + 1, tm), :] * cw[2:3] + cb_ref[ci])
        act = (_silu(conv[:, :fc]) * conv[:, fc:]).astype(BF16)
        acc_ref[...] += jnp.dot(act, wout_ref[ci], preferred_element_type=F32)

    acc_ref[...] = jnp.zeros_like(acc_ref)
    project(0, a0_ref)

    def pair_body(j, carry):
        project(2 * j + 1, a1_ref)
        mix(2 * j, a0_ref)
        project(2 * j + 2, a0_ref)
        mix(2 * j + 1, a1_ref)
        return carry

    assert nch % 2 == 1
    lax.fori_loop(0, nch // 2, pair_body, 0)
    mix(nch - 1, a0_ref)
    o_ref[0] = x_ref[0] + g2_ref[0] * _rms(acc_ref[...], ng_ref[...])


def _conv_ffn(x, shift, scale, gain, win_c, cw_c, cb_c, wout_c, gate, gain_out):
    b, n, d = x.shape
    tm = _token_tile(n, 512)
    assert tm % HALO == 0
    hb = tm // HALO
    n_hb = n // HALO
    row = lambda bi, i: (bi, 0, 0)
    tile = lambda bi, i: (bi, i, 0)
    return pl.pallas_call(
        _ffn_kernel,
        out_shape=jax.ShapeDtypeStruct((b, n, d), F32),
        grid=(b, n // tm),
        in_specs=[pl.BlockSpec((1, HALO, d), lambda bi, i: (bi, jnp.maximum(i * hb - 1, 0), 0)),
                  pl.BlockSpec((1, tm, d), tile),
                  pl.BlockSpec((1, HALO, d), lambda bi, i: (bi, jnp.minimum((i + 1) * hb, n_hb - 1), 0)),
                  pl.BlockSpec((1, 1, d), row),
                  pl.BlockSpec((1, 1, d), row),
                  _resident((1, d)),
                  _resident(win_c.shape),
                  _resident(cw_c.shape),
                  _resident(cb_c.shape),
                  _resident(wout_c.shape),
                  pl.BlockSpec((1, 1, d), row),
                  _resident((1, d))],
        out_specs=pl.BlockSpec((1, tm, d), tile),
        scratch_shapes=[pltpu.VMEM((tm + 2 * HALO, d), BF16),
                        pltpu.VMEM((tm + 2 * HALO, win_c.shape[2]), F32),
                        pltpu.VMEM((tm + 2 * HALO, win_c.shape[2]), F32),
                        pltpu.VMEM((tm, d), F32)],
        compiler_params=_params(2),
        name="conv_ffn",
    )(x, x, x, shift, scale, gain, win_c, cw_c, cb_c, wout_c, gate, gain_out)


def _ffn_weights(w_in, conv_w, conv_b, w_out):
    d, two_ff = w_in.shape
    ff = two_ff // 2
    nch = ff // FFN_CHUNK
    assert nch * FFN_CHUNK == ff
    pair = lambda a: jnp.concatenate([a[..., :ff].reshape(a.shape[:-1] + (nch, FFN_CHUNK)),
                                      a[..., ff:].reshape(a.shape[:-1] + (nch, FFN_CHUNK))], axis=-1)
    win_c = pair(w_in).transpose(1, 0, 2).astype(BF16)
    cw_c = pair(conv_w).transpose(1, 0, 2)
    cb_c = pair(conv_b)[:, None, :]
    wout_c = w_out.reshape(nch, FFN_CHUNK, w_out.shape[1]).astype(BF16)
    return win_c, cw_c, cb_c, wout_c


def _rope_tables(t, dk):
    tok = jnp.arange(t)
    row = (tok // GRID_W).astype(F32)
    col = (tok % GRID_W).astype(F32)
    n_freq = dk // 4
    inv = ROPE_BASE ** (-jnp.arange(n_freq, dtype=F32) / n_freq)
    ang = jnp.concatenate([row[:, None] * inv, col[:, None] * inv], axis=-1)
    cos, sin = jnp.cos(ang), jnp.sin(ang)
    return jnp.concatenate([cos, cos], axis=-1), jnp.concatenate([-sin, sin], axis=-1)


def kernel(x, c, ctx, c_ctx, ada_w, ada_b, norm_g, hyb_w_in, na_rpb, sgu_w, sgu_b, hyb_w_out, ret_w_in, ret_log_decay,
           ret_gn_g, ret_w_out, ffn_w_in, ffn_conv_w, ffn_conv_b, ffn_w_out):
    b, t, d = x.shape
    depth = ada_w.shape[0]
    dk = d // RET_HEADS
    dv = 2 * dk

    cond = jnp.zeros((8, d), F32).at[:b].set(c).at[b].set(c_ctx)
    mods = _ada_rows(cond, ada_w, ada_b)
    cos2, sin2 = _rope_tables(t, dk)
    xc = ctx

    for i in range(depth):
        ctx_out = i < depth - 1
        j = i // 2
        lat = [mods[i, :b, m * d:(m + 1) * d][:, None, :] for m in range(6)]
        cxm = [jnp.broadcast_to(mods[i, b, m * d:(m + 1) * d][None, None, :], (b, 1, d)) for m in range(6)]
        gains = [norm_g[i, m][None, :] for m in range(4)]

        if i % 2 == 0:
            w_in = hyb_w_in[j].astype(BF16)
            w_out = hyb_w_out[j].astype(BF16)
            ws_pairs = sgu_w[j].astype(BF16).reshape(SGU_GROUPS // 2, 2 * SGU_CHUNK, SGU_CHUNK)
            bs_tab = jnp.repeat(sgu_b[j].T, (d - NA_WIDTH) // SGU_GROUPS, axis=1)
            bias = _na_bias_tables(na_rpb[j])
            qkv_c, u_c, vn_c = _hyb_in(xc, cxm[0], cxm[1], gains[0], w_in)
            qkv_l, u_l, vn_l = _hyb_in(x, lat[0], lat[1], gains[0], w_in)
            a_l = _na_attention(qkv_l, qkv_c, bias)
            x = _hyb_out(a_l, u_l, vn_l, ws_pairs, bs_tab, w_out, x, lat[2], gains[1])
            if ctx_out:
                a_c = _ctx_attention(qkv_c)
                xc = _hyb_out(a_c, u_c, vn_c, ws_pairs, bs_tab, w_out, xc, cxm[2], gains[1])
        else:
            w_in = ret_w_in[j].astype(BF16)
            w_out = ret_w_out[j].astype(BF16)
            lg = ret_log_decay[j].astype(F32)
            gn = ret_gn_g[j][None, :]
            zero_state = jnp.zeros((b, RET_HEADS, dk, dv), F32)
            p_c = _ret_in(xc, cxm[0], cxm[1], gains[0], w_in, cos2, sin2, rope=False)
            p_l = _ret_in(x, lat[0], lat[1], gains[0], w_in, cos2, sin2, rope=True)
            on_c, sf, sb = _ret_scan(p_c, lg, zero_state, zero_state, dk, dv)
            on_l, _, _ = _ret_scan(p_l, lg, sf, sb, dk, dv)
            x = _ret_out(p_l, on_l, gn, w_out, x, lat[2], gains[1])
            if ctx_out:
                xc = _ret_out(p_c, on_c, gn, w_out, xc, cxm[2], gains[1])

        ffn_w = _ffn_weights(ffn_w_in[i], ffn_conv_w[i], ffn_conv_b[i], ffn_w_out[i])
        x = _conv_ffn(x, lat[3], lat[4], gains[2], *ffn_w, lat[5], gains[3])
        if ctx_out:
            xc = _conv_ffn(xc, cxm[3], cxm[4], gains[2], *ffn_w, cxm[5], gains[3])
    return x
```

```python
import functools

import jax
import jax.numpy as jnp
from jax import lax
from jax.experimental import pallas as pl
from jax.experimental.pallas import tpu as pltpu

F32 = jnp.float32
BF16 = jnp.bfloat16

GRID_W = 64
EPS = 1e-6
NEG_INF = -1e30

NA_HEADS = 8
NA_HEAD_DIM = 64
NA_WIDTH = NA_HEADS * NA_HEAD_DIM
NA_KH = 8
NA_KW = 16
NA_GROUP = 4
SGU_GROUPS = 8
SGU_CHUNK = 128

RET_HEADS = 8
RET_CHUNK = 128
ROPE_BASE = 10000.0

CONV_W = 3
FFN_CHUNK = 256
HALO = 16

V7X_VMEM_LIMIT_BYTES = 56 * 1024 * 1024

NT_DIMS = (((1,), (1,)), ((), ()))
TN_DIMS = (((0,), (0,)), ((), ()))


def _params(n_axes):
    return pltpu.CompilerParams(dimension_semantics=("arbitrary",) * n_axes,
                                vmem_limit_bytes=V7X_VMEM_LIMIT_BYTES)


def _resident(shape):
    zeros = (0,) * len(shape)
    return pl.BlockSpec(shape, lambda *_: zeros, pipeline_mode=pl.Buffered(1))


def _token_tile(n, want):
    tm = min(n, want)
    assert n % tm == 0
    return tm


def _rms(v, gain):
    return v * lax.rsqrt(jnp.mean(v * v, axis=-1, keepdims=True) + EPS) * gain


def _silu(v):
    return v * (1.0 / (1.0 + jnp.exp(-v)))


def _norm_mod(x, gain, shift, scale):
    return _rms(x, gain) * (1.0 + scale) + shift


def _ada_kernel(c_ref, w_ref, b_ref, o_ref):
    s = _silu(c_ref[...])
    o_ref[0] = jnp.dot(s, w_ref[0], preferred_element_type=F32) + b_ref[0]


def _ada_rows(cond, ada_w, ada_b):
    depth, d, n = ada_w.shape
    tn = 2048
    return pl.pallas_call(
        _ada_kernel,
        out_shape=jax.ShapeDtypeStruct((depth, 8, n), F32),
        grid=(depth, n // tn),
        in_specs=[pl.BlockSpec((8, d), lambda i, j: (0, 0)),
                  pl.BlockSpec((1, d, tn), lambda i, j: (i, 0, j)),
                  pl.BlockSpec((1, 1, tn), lambda i, j: (i, 0, j))],
        out_specs=pl.BlockSpec((1, 8, tn), lambda i, j: (i, 0, j)),
        compiler_params=_params(2),
        name="ada_rows",
    )(cond, ada_w, ada_b.reshape(depth, 1, n))


def _hyb_in_kernel(x_ref, sh_ref, sc_ref, g_ref, w_ref, qkv_ref, u_ref, vn_ref):
    hb = _norm_mod(x_ref[0], g_ref[...], sh_ref[0], sc_ref[0]).astype(BF16)
    w = NA_WIDTH
    for part in range(3):
        acc = jnp.dot(hb, w_ref[:, part * w:(part + 1) * w], preferred_element_type=F32)
        if part == 0:
            acc = acc * (NA_HEAD_DIM ** -0.5)
        qkv_ref[0, :, part * w:(part + 1) * w] = acc.astype(BF16)
    u = jax.nn.gelu(jnp.dot(hb, w_ref[:, 3 * w:4 * w], preferred_element_type=F32))
    u_ref[0] = u.astype(BF16)
    g = jax.nn.gelu(jnp.dot(hb, w_ref[:, 4 * w:5 * w], preferred_element_type=F32))
    mu = jnp.mean(g, axis=-1, keepdims=True)
    gc = g - mu
    var = jnp.mean(gc * gc, axis=-1, keepdims=True)
    vn_ref[0] = (gc * lax.rsqrt(var + EPS)).astype(BF16)


def _hyb_in(x, shift, scale, gain, w_bf):
    b, n, d = x.shape
    tm = _token_tile(n, 512)
    w = NA_WIDTH
    row = lambda bi, i: (bi, 0, 0)
    tile = lambda bi, i: (bi, i, 0)
    return pl.pallas_call(
        _hyb_in_kernel,
        out_shape=(jax.ShapeDtypeStruct((b, n, 3 * w), BF16),
                   jax.ShapeDtypeStruct((b, n, w), BF16),
                   jax.ShapeDtypeStruct((b, n, w), BF16)),
        grid=(b, n // tm),
        in_specs=[pl.BlockSpec((1, tm, d), tile),
                  pl.BlockSpec((1, 1, d), row),
                  pl.BlockSpec((1, 1, d), row),
                  _resident((1, d)),
                  _resident(w_bf.shape)],
        out_specs=(pl.BlockSpec((1, tm, 3 * w), tile),
                   pl.BlockSpec((1, tm, w), tile),
                   pl.BlockSpec((1, tm, w), tile)),
        compiler_params=_params(2),
        name="hyb_in",
    )(x, shift, scale, gain, w_bf)


def _split_heads(qg):
    lane = lax.broadcasted_iota(jnp.int32, qg.shape, 1)
    zero = jnp.zeros_like(qg)
    return jnp.concatenate([jnp.where(lane // NA_HEAD_DIM == hd, qg, zero) for hd in range(NA_GROUP)], axis=0)


def _merge_heads(og):
    m = og.shape[0] // NA_GROUP
    lane = lax.broadcasted_iota(jnp.int32, (m, og.shape[1]), 1)
    out = og[:m]
    for hd in range(1, NA_GROUP):
        out = jnp.where(lane // NA_HEAD_DIM == hd, og[hd * m:(hd + 1) * m], out)
    return out


def _fold_lanes(v, op, width=128):
    out = v[:, :width]
    for c0 in range(width, v.shape[1], width):
        out = op(out, v[:, c0:c0 + width])
    return out


def _na_kernel(q_ref, k_ref, v_ref, kc_ref, vc_ref, bias_ref, o_ref, *, rows):
    kh = NA_KH
    kc = kc_ref[0]
    vc = vc_ref[0]

    def row_body(r, carry):
        r0 = jnp.clip(r - kh // 2, 0, rows - kh)
        q_off = pl.multiple_of(r * GRID_W, GRID_W)
        k_off = pl.multiple_of(r0 * GRID_W, GRID_W)
        qs = _split_heads(q_ref[0, pl.ds(q_off, GRID_W), :])
        kspan = k_ref[0, pl.ds(k_off, kh * GRID_W), :]
        vspan = v_ref[0, pl.ds(k_off, kh * GRID_W), :]
        s_nb = lax.dot_general(qs, kspan, NT_DIMS, preferred_element_type=F32) + bias_ref[0, r - r0]
        s_cx = lax.dot_general(qs, kc, NT_DIMS, preferred_element_type=F32)
        m = jnp.max(jnp.maximum(_fold_lanes(s_nb, jnp.maximum), _fold_lanes(s_cx, jnp.maximum)),
                    axis=-1, keepdims=True)
        p_nb = jnp.exp(s_nb - m)
        p_cx = jnp.exp(s_cx - m)
        denom = jnp.sum(_fold_lanes(p_nb, jnp.add) + _fold_lanes(p_cx, jnp.add), axis=-1, keepdims=True)
        o2 = (jnp.dot(p_nb.astype(BF16), vspan, preferred_element_type=F32)
              + jnp.dot(p_cx.astype(BF16), vc, preferred_element_type=F32)) / denom
        o_ref[0, pl.ds(q_off, GRID_W), :] = _merge_heads(o2).astype(BF16)
        return carry

    lax.fori_loop(0, rows, row_body, 0, unroll=4)


def _na_bias_tables(rpb):
    h = rpb.shape[0]
    w, kw = GRID_W, NA_KW
    cidx = jnp.arange(w)
    c0 = jnp.clip(cidx - kw // 2, 0, w - kw)
    in_win = (cidx[None, :] >= c0[:, None]) & (cidx[None, :] < c0[:, None] + kw)
    edge = w - kw
    ext = jnp.concatenate([jnp.broadcast_to(rpb[..., :1], rpb.shape[:-1] + (edge,)), rpb.astype(F32),
                           jnp.broadcast_to(rpb[..., -1:], rpb.shape[:-1] + (edge,))], axis=-1)
    toep = jnp.stack([ext[..., w - 1 - q:2 * w - 1 - q] for q in range(w)], axis=-2)
    toep = jnp.where(in_win, toep, NEG_INF)
    tab = jnp.stack([toep[:, NA_KH - 1 - o:2 * NA_KH - 1 - o] for o in range(NA_KH)], axis=1)
    g = NA_GROUP
    tab = tab.reshape(h // g, g, NA_KH, NA_KH, w, w).transpose(0, 2, 1, 4, 3, 5)
    return tab.reshape(h // g, NA_KH, g * w, NA_KH * w)


def _na_attention(qkv, qkv_ctx, bias):
    b, t, _ = qkv.shape
    l = qkv_ctx.shape[1]
    rows = t // GRID_W
    assert rows >= NA_KH
    groups = NA_HEADS // NA_GROUP
    lanes = NA_GROUP * NA_HEAD_DIM
    return pl.pallas_call(
        functools.partial(_na_kernel, rows=rows),
        out_shape=jax.ShapeDtypeStruct((b, t, NA_WIDTH), BF16),
        grid=(b, groups),
        in_specs=[pl.BlockSpec((1, t, lanes), lambda bi, hg: (bi, 0, hg)),
                  pl.BlockSpec((1, t, lanes), lambda bi, hg: (bi, 0, groups + hg)),
                  pl.BlockSpec((1, t, lanes), lambda bi, hg: (bi, 0, 2 * groups + hg)),
                  pl.BlockSpec((1, l, lanes), lambda bi, hg: (bi, 0, groups + hg)),
                  pl.BlockSpec((1, l, lanes), lambda bi, hg: (bi, 0, 2 * groups + hg)),
                  pl.BlockSpec((1, NA_KH, NA_GROUP * GRID_W, NA_KH * GRID_W), lambda bi, hg: (hg, 0, 0, 0))],
        out_specs=pl.BlockSpec((1, t, lanes), lambda bi, hg: (bi, 0, hg)),
        compiler_params=_params(2),
        name="na_attention",
    )(qkv, qkv, qkv, qkv_ctx, qkv_ctx, bias)


def _ctx_attn_kernel(q_ref, k_ref, v_ref, o_ref):
    qs = _split_heads(q_ref[0])
    s = lax.dot_general(qs, k_ref[0], NT_DIMS, preferred_element_type=F32)
    p = jnp.exp(s - jnp.max(s, axis=-1, keepdims=True))
    denom = jnp.sum(p, axis=-1, keepdims=True)
    o2 = jnp.dot(p.astype(BF16), v_ref[0], preferred_element_type=F32) / denom
    o_ref[0] = _merge_heads(o2).astype(BF16)


def _ctx_attention(qkv_ctx):
    b, l, _ = qkv_ctx.shape
    groups = NA_HEADS // NA_GROUP
    lanes = NA_GROUP * NA_HEAD_DIM
    return pl.pallas_call(
        _ctx_attn_kernel,
        out_shape=jax.ShapeDtypeStruct((b, l, NA_WIDTH), BF16),
        grid=(b, groups),
        in_specs=[pl.BlockSpec((1, l, lanes), lambda bi, hg: (bi, 0, hg)),
                  pl.BlockSpec((1, l, lanes), lambda bi, hg: (bi, 0, groups + hg)),
                  pl.BlockSpec((1, l, lanes), lambda bi, hg: (bi, 0, 2 * groups + hg))],
        out_specs=pl.BlockSpec((1, l, lanes), lambda bi, hg: (bi, 0, hg)),
        compiler_params=_params(2),
        name="ctx_attention",
    )(qkv_ctx, qkv_ctx, qkv_ctx)


def _hyb_out_kernel(a_ref, u_ref, vn_ref, ws_ref, bs_ref, w_ref, x_ref, g1_ref, ng_ref, o_ref, s_ref):
    tm = a_ref.shape[1]
    c = SGU_CHUNK
    gd = vn_ref.shape[2] // SGU_GROUPS
    for ci in range(tm // c):
        rows = slice(ci * c, (ci + 1) * c)
        for gp in range(SGU_GROUPS // 2):
            lanes = slice(gp * 2 * gd, (gp + 1) * 2 * gd)
            mixed2 = jnp.dot(ws_ref[gp], vn_ref[0, rows, lanes], preferred_element_type=F32)
            lane = lax.broadcasted_iota(jnp.int32, (c, 2 * gd), 1)
            mixed = jnp.where(lane < gd, mixed2[:c], mixed2[c:]) + bs_ref[:, lanes]
            s_ref[rows, lanes] = (u_ref[0, rows, lanes].astype(F32) * mixed).astype(BF16)
    wa = a_ref.shape[2]
    y = (jnp.dot(a_ref[0], w_ref[:wa, :], preferred_element_type=F32)
         + jnp.dot(s_ref[...], w_ref[wa:, :], preferred_element_type=F32))
    o_ref[0] = x_ref[0] + g1_ref[0] * _rms(y, ng_ref[...])


def _hyb_out(a, u, vn, ws_pairs, bs_tab, w_bf, x, gate, gain):
    b, n, d = x.shape
    tm = _token_tile(n, 512)
    w = a.shape[2]
    row = lambda bi, i: (bi, 0, 0)
    tile = lambda bi, i: (bi, i, 0)
    return pl.pallas_call(
        _hyb_out_kernel,
        out_shape=jax.ShapeDtypeStruct((b, n, d), F32),
        grid=(b, n // tm),
        in_specs=[pl.BlockSpec((1, tm, w), tile),
                  pl.BlockSpec((1, tm, w), tile),
                  pl.BlockSpec((1, tm, w), tile),
                  _resident(ws_pairs.shape),
                  _resident(bs_tab.shape),
                  _resident(w_bf.shape),
                  pl.BlockSpec((1, tm, d), tile),
                  pl.BlockSpec((1, 1, d), row),
                  _resident((1, d))],
        out_specs=pl.BlockSpec((1, tm, d), tile),
        scratch_shapes=[pltpu.VMEM((tm, w), BF16)],
        compiler_params=_params(2),
        name="hyb_out",
    )(a, u, vn, ws_pairs, bs_tab, w_bf, x, gate, gain)


def _ret_in_kernel(x_ref, sh_ref, sc_ref, g_ref, w_ref, cos_ref, sin_ref, o_ref, *, rope, dk):
    hb = _norm_mod(x_ref[0], g_ref[...], sh_ref[0], sc_ref[0]).astype(BF16)
    qk_w = RET_HEADS * dk
    n_out = o_ref.shape[2]
    step = 512
    for c0 in range(0, n_out, step):
        acc = jnp.dot(hb, w_ref[:, c0:c0 + step], preferred_element_type=F32)
        if c0 < 2 * qk_w:
            if c0 >= qk_w:
                acc = acc * (dk ** -0.5)
            if rope:
                parts = []
                for hd in range(step // dk):
                    t = acc[:, hd * dk:(hd + 1) * dk]
                    parts.append(t * cos_ref[...] + pltpu.roll(t, dk // 2, 1) * sin_ref[...])
                acc = jnp.concatenate(parts, axis=-1)
        elif c0 >= 2 * qk_w + (n_out - 2 * qk_w) // 2:
            acc = _silu(acc)
        o_ref[0, :, c0:c0 + step] = acc.astype(BF16)


def _ret_in(x, shift, scale, gain, w_bf, cos2, sin2, rope):
    b, n, d = x.shape
    tm = _token_tile(n, 512)
    n_out = w_bf.shape[1]
    dk = cos2.shape[1]
    row = lambda bi, i: (bi, 0, 0)
    tile = lambda bi, i: (bi, i, 0)
    return pl.pallas_call(
        functools.partial(_ret_in_kernel, rope=rope, dk=dk),
        out_shape=jax.ShapeDtypeStruct((b, n, n_out), BF16),
        grid=(b, n // tm),
        in_specs=[pl.BlockSpec((1, tm, d), tile),
                  pl.BlockSpec((1, 1, d), row),
                  pl.BlockSpec((1, 1, d), row),
                  _resident((1, d)),
                  _resident(w_bf.shape),
                  pl.BlockSpec((tm, dk), lambda bi, i: (i, 0)),
                  pl.BlockSpec((tm, dk), lambda bi, i: (i, 0))],
        out_specs=pl.BlockSpec((1, tm, n_out), tile),
        compiler_params=_params(2),
        name="ret_in",
    )(x, shift, scale, gain, w_bf, cos2, sin2)


def _ret_scan_kernel(lg_ref, q_ref, k_ref, v_ref, s0f_ref, s0b_ref, on_ref, sf_ref, sb_ref, pre_ref):
    head = pl.program_id(1)
    n, dv = v_ref.shape[1], v_ref.shape[2]
    c = RET_CHUNK
    nc = n // c
    lgf = lg_ref[0, head]
    lgb = lg_ref[1, head]
    pos = lax.broadcasted_iota(jnp.int32, (c, 1), 0).astype(F32)
    diff = (lax.broadcasted_iota(jnp.int32, (c, c), 0) - lax.broadcasted_iota(jnp.int32, (c, c), 1)).astype(F32)
    lower, upper = diff >= 0, diff <= 0
    decay = (jnp.where(lower, jnp.exp(lgf * jnp.where(lower, diff, 0.0)), 0.0)
             + jnp.where(upper, jnp.exp(lgb * jnp.where(upper, -diff, 0.0)), 0.0))
    dq_f, dk_f, dc_f = jnp.exp(lgf * (pos + 1.0)), jnp.exp(lgf * (c - 1.0 - pos)), jnp.exp(lgf * c)
    dq_b, dk_b, dc_b = jnp.exp(lgb * (c - pos)), jnp.exp(lgb * pos), jnp.exp(lgb * c)

    sf_ref[0, 0] = s0f_ref[0, 0]
    sb_ref[0, 0] = s0b_ref[0, 0]

    def advance(ci, st_ref, lanes, dk_row, dchunk):
        off = pl.multiple_of(ci * c, c)
        kd = (k_ref[0, pl.ds(off, c), :].astype(F32) * dk_row).astype(BF16)
        upd = lax.dot_general(kd, v_ref[0, pl.ds(off, c), :], TN_DIMS, preferred_element_type=F32)
        state = st_ref[0, 0]
        pre_ref[ci, :, lanes] = state.astype(BF16)
        st_ref[0, 0] = state * dchunk + upd

    def state_body(i, carry):
        advance(i, sf_ref, slice(0, dv), dk_f, dc_f)
        advance(nc - 1 - i, sb_ref, slice(dv, 2 * dv), dk_b, dc_b)
        return carry

    lax.fori_loop(0, nc, state_body, 0, unroll=min(4, nc))

    def out_body(ci, carry):
        off = pl.multiple_of(ci * c, c)
        q = q_ref[0, pl.ds(off, c), :]
        k = k_ref[0, pl.ds(off, c), :]
        v = v_ref[0, pl.ds(off, c), :]
        s = lax.dot_general(q, k, NT_DIMS, preferred_element_type=F32) * decay
        cross = jnp.dot(q, pre_ref[ci], preferred_element_type=F32)
        tot = (jnp.dot(s.astype(BF16), v, preferred_element_type=F32)
               + dq_f * cross[:, :dv] + dq_b * cross[:, dv:])
        mu = jnp.mean(tot, axis=-1, keepdims=True)
        tc = tot - mu
        var = jnp.mean(tc * tc, axis=-1, keepdims=True)
        on_ref[0, pl.ds(off, c), :] = (tc * lax.rsqrt(var + EPS)).astype(BF16)
        return carry

    lax.fori_loop(0, nc, out_body, 0, unroll=min(8, nc))


def _ret_scan(qkvg, log_decay, s0f, s0b, dk, dv):
    b, n, _ = qkvg.shape
    h = RET_HEADS
    v_blk0 = (2 * h * dk) // dv
    state = jax.ShapeDtypeStruct((b, h, dk, dv), F32)
    st_spec = pl.BlockSpec((1, 1, dk, dv), lambda bi, hi: (bi, hi, 0, 0))
    return pl.pallas_call(
        _ret_scan_kernel,
        out_shape=(jax.ShapeDtypeStruct((b, n, h * dv), BF16), state, state),
        grid=(b, h),
        in_specs=[pl.BlockSpec(memory_space=pltpu.SMEM),
                  pl.BlockSpec((1, n, dk), lambda bi, hi: (bi, 0, hi)),
                  pl.BlockSpec((1, n, dk), lambda bi, hi: (bi, 0, h + hi)),
                  pl.BlockSpec((1, n, dv), lambda bi, hi: (bi, 0, v_blk0 + hi)),
                  st_spec, st_spec],
        out_specs=(pl.BlockSpec((1, n, dv), lambda bi, hi: (bi, 0, hi)), st_spec, st_spec),
        scratch_shapes=[pltpu.VMEM((n // RET_CHUNK, dk, 2 * dv), BF16)],
        compiler_params=_params(2),
        name="ret_scan",
    )(log_decay, qkvg, qkvg, qkvg, s0f, s0b)


def _ret_out_kernel(gt_ref, on_ref, gn_ref, w_ref, x_ref, g1_ref, ng_ref, o_ref):
    z = (gt_ref[0].astype(F32) * (on_ref[0].astype(F32) * gn_ref[...])).astype(BF16)
    y = jnp.dot(z, w_ref[...], preferred_element_type=F32)
    o_ref[0] = x_ref[0] + g1_ref[0] * _rms(y, ng_ref[...])


def _ret_out(qkvg, on, gn_g, w_bf, x, gate, gain):
    b, n, d = x.shape
    tm = _token_tile(n, 512)
    vw = on.shape[2]
    g_blk = qkvg.shape[2] // vw - 1
    row = lambda bi, i: (bi, 0, 0)
    tile = lambda bi, i: (bi, i, 0)
    return pl.pallas_call(
        _ret_out_kernel,
        out_shape=jax.ShapeDtypeStruct((b, n, d), F32),
        grid=(b, n // tm),
        in_specs=[pl.BlockSpec((1, tm, vw), lambda bi, i: (bi, i, g_blk)),
                  pl.BlockSpec((1, tm, vw), tile),
                  _resident((1, vw)),
                  _resident(w_bf.shape),
                  pl.BlockSpec((1, tm, d), tile),
                  pl.BlockSpec((1, 1, d), row),
                  _resident((1, d))],
        out_specs=pl.BlockSpec((1, tm, d), tile),
        compiler_params=_params(2),
        name="ret_out",
    )(qkvg, on, gn_g, w_bf, x, gate, gain)


def _ffn_kernel(xp_ref, x_ref, xn_ref, sh_ref, sc_ref, g_ref, win_ref, cw_ref, cb_ref, wout_ref, g2_ref, ng_ref,
                o_ref, h_ref, a0_ref, a1_ref, acc_ref):
    i = pl.program_id(1)
    last = pl.num_programs(1) - 1
    tm = x_ref.shape[1]
    gain, shift, scale = g_ref[...], sh_ref[0], sc_ref[0]
    hp = jnp.where(i > 0, _norm_mod(xp_ref[0], gain, shift, scale), 0.0)
    hn = jnp.where(i < last, _norm_mod(xn_ref[0], gain, shift, scale), 0.0)
    h_ref[:HALO] = hp.astype(BF16)
    h_ref[HALO:HALO + tm] = _norm_mod(x_ref[0], gain, shift, scale).astype(BF16)
    h_ref[HALO + tm:] = hn.astype(BF16)
    fc = wout_ref.shape[1]

    nch = win_ref.shape[0]

    def project(ci, a_ref):
        a_ref[...] = jnp.dot(h_ref[...], win_ref[ci], preferred_element_type=F32)

    def mix(ci, a_ref):
        cw = cw_ref[ci]
        conv = (a_ref[pl.ds(HALO - 1, tm), :] * cw[0:1] + a_ref[pl.ds(HALO, tm), :] * cw[1:2]
                + a_ref[pl.ds(HALO + 1, tm), :] * cw[2:3] + cb_ref[ci])
        act = (_silu(conv[:, :fc]) * conv[:, fc:]).astype(BF16)
        acc_ref[...] += jnp.dot(act, wout_ref[ci], preferred_element_type=F32)

    acc_ref[...] = jnp.zeros_like(acc_ref)
    project(0, a0_ref)

    def pair_body(j, carry):
        project(2 * j + 1, a1_ref)
        mix(2 * j, a0_ref)
        project(2 * j + 2, a0_ref)
        mix(2 * j + 1, a1_ref)
        return carry

    assert nch % 2 == 1
    lax.fori_loop(0, nch // 2, pair_body, 0)
    mix(nch - 1, a0_ref)
    o_ref[0] = x_ref[0] + g2_ref[0] * _rms(acc_ref[...], ng_ref[...])


def _conv_ffn(x, shift, scale, gain, win_c, cw_c, cb_c, wout_c, gate, gain_out):
    b, n, d = x.shape
    tm = _token_tile(n, 1024)
    assert tm % HALO == 0
    hb = tm // HALO
    n_hb = n // HALO
    row = lambda bi, i: (bi, 0, 0)
    tile = lambda bi, i: (bi, i, 0)
    return pl.pallas_call(
        _ffn_kernel,
        out_shape=jax.ShapeDtypeStruct((b, n, d), F32),
        grid=(b, n // tm),
        in_specs=[pl.BlockSpec((1, HALO, d), lambda bi, i: (bi, jnp.maximum(i * hb - 1, 0), 0)),
                  pl.BlockSpec((1, tm, d), tile),
                  pl.BlockSpec((1, HALO, d), lambda bi, i: (bi, jnp.minimum((i + 1) * hb, n_hb - 1), 0)),
                  pl.BlockSpec((1, 1, d), row),
                  pl.BlockSpec((1, 1, d), row),
                  _resident((1, d)),
                  _resident(win_c.shape),
                  _resident(cw_c.shape),
                  _resident(cb_c.shape),
                  _resident(wout_c.shape),
                  pl.BlockSpec((1, 1, d), row),
                  _resident((1, d))],
        out_specs=pl.BlockSpec((1, tm, d), tile),
        scratch_shapes=[pltpu.VMEM((tm + 2 * HALO, d), BF16),
                        pltpu.VMEM((tm + 2 * HALO, win_c.shape[2]), F32),
                        pltpu.VMEM((tm + 2 * HALO, win_c.shape[2]), F32),
                        pltpu.VMEM((tm, d), F32)],
        compiler_params=_params(2),
        name="conv_ffn",
    )(x, x, x, shift, scale, gain, win_c, cw_c, cb_c, wout_c, gate, gain_out)


def _ffn_weights(w_in, conv_w, conv_b, w_out):
    d, two_ff = w_in.shape
    ff = two_ff // 2
    nch = ff // FFN_CHUNK
    assert nch * FFN_CHUNK == ff
    pair = lambda a: jnp.concatenate([a[..., :ff].reshape(a.shape[:-1] + (nch, FFN_CHUNK)),
                                      a[..., ff:].reshape(a.shape[:-1] + (nch, FFN_CHUNK))], axis=-1)
    win_c = pair(w_in).transpose(1, 0, 2).astype(BF16)
    cw_c = pair(conv_w).transpose(1, 0, 2)
    cb_c = pair(conv_b)[:, None, :]
    wout_c = w_out.reshape(nch, FFN_CHUNK, w_out.shape[1]).astype(BF16)
    return win_c, cw_c, cb_c, wout_c


def _rope_tables(t, dk):
    tok = jnp.arange(t)
    row = (tok // GRID_W).astype(F32)
    col = (tok % GRID_W).astype(F32)
    n_freq = dk // 4
    inv = ROPE_BASE ** (-jnp.arange(n_freq, dtype=F32) / n_freq)
    ang = jnp.concatenate([row[:, None] * inv, col[:, None] * inv], axis=-1)
    cos, sin = jnp.cos(ang), jnp.sin(ang)
    return jnp.concatenate([cos, cos], axis=-1), jnp.concatenate([-sin, sin], axis=-1)


def kernel(x, c, ctx, c_ctx, ada_w, ada_b, norm_g, hyb_w_in, na_rpb, sgu_w, sgu_b, hyb_w_out, ret_w_in, ret_log_decay,
           ret_gn_g, ret_w_out, ffn_w_in, ffn_conv_w, ffn_conv_b, ffn_w_out):
    b, t, d = x.shape
    depth = ada_w.shape[0]
    dk = d // RET_HEADS
    dv = 2 * dk

    cond = jnp.zeros((8, d), F32).at[:b].set(c).at[b].set(c_ctx)
    mods = _ada_rows(cond, ada_w, ada_b)
    cos2, sin2 = _rope_tables(t, dk)
    xc = ctx

    for i in range(depth):
        ctx_out = i < depth - 1
        j = i // 2
        lat = [mods[i, :b, m * d:(m + 1) * d][:, None, :] for m in range(6)]
        cxm = [jnp.broadcast_to(mods[i, b, m * d:(m + 1) * d][None, None, :], (b, 1, d)) for m in range(6)]
        gains = [norm_g[i, m][None, :] for m in range(4)]

        if i % 2 == 0:
            w_in = hyb_w_in[j].astype(BF16)
            w_out = hyb_w_out[j].astype(BF16)
            ws_pairs = sgu_w[j].astype(BF16).reshape(SGU_GROUPS // 2, 2 * SGU_CHUNK, SGU_CHUNK)
            bs_tab = jnp.repeat(sgu_b[j].T, (d - NA_WIDTH) // SGU_GROUPS, axis=1)
            bias = _na_bias_tables(na_rpb[j])
            qkv_c, u_c, vn_c = _hyb_in(xc, cxm[0], cxm[1], gains[0], w_in)
            qkv_l, u_l, vn_l = _hyb_in(x, lat[0], lat[1], gains[0], w_in)
            a_l = _na_attention(qkv_l, qkv_c, bias)
            x = _hyb_out(a_l, u_l, vn_l, ws_pairs, bs_tab, w_out, x, lat[2], gains[1])
            if ctx_out:
                a_c = _ctx_attention(qkv_c)
                xc = _hyb_out(a_c, u_c, vn_c, ws_pairs, bs_tab, w_out, xc, cxm[2], gains[1])
        else:
            w_in = ret_w_in[j].astype(BF16)
            w_out = ret_w_out[j].astype(BF16)
            lg = ret_log_decay[j].astype(F32)
            gn = ret_gn_g[j][None, :]
            zero_state = jnp.zeros((b, RET_HEADS, dk, dv), F32)
            p_c = _ret_in(xc, cxm[0], cxm[1], gains[0], w_in, cos2, sin2, rope=False)
            p_l = _ret_in(x, lat[0], lat[1], gains[0], w_in, cos2, sin2, rope=True)
            on_c, sf, sb = _ret_scan(p_c, lg, zero_state, zero_state, dk, dv)
            on_l, _, _ = _ret_scan(p_l, lg, sf, sb, dk, dv)
            x = _ret_out(p_l, on_l, gn, w_out, x, lat[2], gains[1])
            if ctx_out:
                xc = _ret_out(p_c, on_c, gn, w_out, xc, cxm[2], gains[1])

        ffn_w = _ffn_weights(ffn_w_in[i], ffn_conv_w[i], ffn_conv_b[i], ffn_w_out[i])
        x = _conv_ffn(x, lat[3], lat[4], gains[2], *ffn_w, lat[5], gains[3])
        if ctx_out:
            xc = _conv_ffn(xc, cxm[3], cxm[4], gains[2], *ffn_w, cxm[5], gains[3])
    return x
```

```python
import functools

import jax
import jax.numpy as jnp
from jax import lax
from jax.experimental import pallas as pl
from jax.experimental.pallas import tpu as pltpu

F32 = jnp.float32
BF16 = jnp.bfloat16

GRID_W = 64
EPS = 1e-6
NEG_INF = -1e30

NA_HEADS = 8
NA_HEAD_DIM = 64
NA_WIDTH = NA_HEADS * NA_HEAD_DIM
NA_KH = 8
NA_KW = 16
NA_GROUP = 4
SGU_GROUPS = 8
SGU_CHUNK = 128

RET_HEADS = 8
RET_CHUNK = 256
ROPE_BASE = 10000.0

CONV_W = 3
FFN_CHUNK = 256
HALO = 16

V7X_VMEM_LIMIT_BYTES = 56 * 1024 * 1024

NT_DIMS = (((1,), (1,)), ((), ()))
TN_DIMS = (((0,), (0,)), ((), ()))


def _params(n_axes):
    return pltpu.CompilerParams(dimension_semantics=("arbitrary",) * n_axes,
                                vmem_limit_bytes=V7X_VMEM_LIMIT_BYTES)


def _resident(shape):
    zeros = (0,) * len(shape)
    return pl.BlockSpec(shape, lambda *_: zeros, pipeline_mode=pl.Buffered(1))


def _token_tile(n, want):
    tm = min(n, want)
    assert n % tm == 0
    return tm


def _rms(v, gain):
    return v * lax.rsqrt(jnp.mean(v * v, axis=-1, keepdims=True) + EPS) * gain


def _silu(v):
    return v * (1.0 / (1.0 + jnp.exp(-v)))


def _norm_mod(x, gain, shift, scale):
    return _rms(x, gain) * (1.0 + scale) + shift


def _ada_kernel(c_ref, w_ref, b_ref, o_ref):
    s = _silu(c_ref[...])
    o_ref[0] = jnp.dot(s, w_ref[0], preferred_element_type=F32) + b_ref[0]


def _ada_rows(cond, ada_w, ada_b):
    depth, d, n = ada_w.shape
    tn = 2048
    return pl.pallas_call(
        _ada_kernel,
        out_shape=jax.ShapeDtypeStruct((depth, 8, n), F32),
        grid=(depth, n // tn),
        in_specs=[pl.BlockSpec((8, d), lambda i, j: (0, 0)),
                  pl.BlockSpec((1, d, tn), lambda i, j: (i, 0, j)),
                  pl.BlockSpec((1, 1, tn), lambda i, j: (i, 0, j))],
        out_specs=pl.BlockSpec((1, 8, tn), lambda i, j: (i, 0, j)),
        compiler_params=_params(2),
        name="ada_rows",
    )(cond, ada_w, ada_b.reshape(depth, 1, n))


def _hyb_in_kernel(x_ref, sh_ref, sc_ref, g_ref, w_ref, qkv_ref, u_ref, vn_ref):
    hb = _norm_mod(x_ref[0], g_ref[...], sh_ref[0], sc_ref[0]).astype(BF16)
    w = NA_WIDTH
    for part in range(3):
        acc = jnp.dot(hb, w_ref[:, part * w:(part + 1) * w], preferred_element_type=F32)
        if part == 0:
            acc = acc * (NA_HEAD_DIM ** -0.5)
        qkv_ref[0, :, part * w:(part + 1) * w] = acc.astype(BF16)
    u = jax.nn.gelu(jnp.dot(hb, w_ref[:, 3 * w:4 * w], preferred_element_type=F32))
    u_ref[0] = u.astype(BF16)
    g = jax.nn.gelu(jnp.dot(hb, w_ref[:, 4 * w:5 * w], preferred_element_type=F32))
    mu = jnp.mean(g, axis=-1, keepdims=True)
    gc = g - mu
    var = jnp.mean(gc * gc, axis=-1, keepdims=True)
    vn_ref[0] = (gc * lax.rsqrt(var + EPS)).astype(BF16)


def _hyb_in(x, shift, scale, gain, w_bf):
    b, n, d = x.shape
    tm = _token_tile(n, 512)
    w = NA_WIDTH
    row = lambda bi, i: (bi, 0, 0)
    tile = lambda bi, i: (bi, i, 0)
    return pl.pallas_call(
        _hyb_in_kernel,
        out_shape=(jax.ShapeDtypeStruct((b, n, 3 * w), BF16),
                   jax.ShapeDtypeStruct((b, n, w), BF16),
                   jax.ShapeDtypeStruct((b, n, w), BF16)),
        grid=(b, n // tm),
        in_specs=[pl.BlockSpec((1, tm, d), tile),
                  pl.BlockSpec((1, 1, d), row),
                  pl.BlockSpec((1, 1, d), row),
                  _resident((1, d)),
                  _resident(w_bf.shape)],
        out_specs=(pl.BlockSpec((1, tm, 3 * w), tile),
                   pl.BlockSpec((1, tm, w), tile),
                   pl.BlockSpec((1, tm, w), tile)),
        compiler_params=_params(2),
        name="hyb_in",
    )(x, shift, scale, gain, w_bf)


def _split_heads(qg):
    lane = lax.broadcasted_iota(jnp.int32, qg.shape, 1)
    zero = jnp.zeros_like(qg)
    return jnp.concatenate([jnp.where(lane // NA_HEAD_DIM == hd, qg, zero) for hd in range(NA_GROUP)], axis=0)


def _merge_heads(og):
    m = og.shape[0] // NA_GROUP
    lane = lax.broadcasted_iota(jnp.int32, (m, og.shape[1]), 1)
    out = og[:m]
    for hd in range(1, NA_GROUP):
        out = jnp.where(lane // NA_HEAD_DIM == hd, og[hd * m:(hd + 1) * m], out)
    return out


def _fold_lanes(v, op, width=128):
    out = v[:, :width]
    for c0 in range(width, v.shape[1], width):
        out = op(out, v[:, c0:c0 + width])
    return out


def _na_kernel(q_ref, k_ref, v_ref, kc_ref, vc_ref, bias_ref, o_ref, *, rows):
    kh = NA_KH
    kc = kc_ref[0]
    vc = vc_ref[0]

    def row_body(r, carry):
        r0 = jnp.clip(r - kh // 2, 0, rows - kh)
        q_off = pl.multiple_of(r * GRID_W, GRID_W)
        k_off = pl.multiple_of(r0 * GRID_W, GRID_W)
        qs = _split_heads(q_ref[0, pl.ds(q_off, GRID_W), :])
        kspan = k_ref[0, pl.ds(k_off, kh * GRID_W), :]
        vspan = v_ref[0, pl.ds(k_off, kh * GRID_W), :]
        s_nb = lax.dot_general(qs, kspan, NT_DIMS, preferred_element_type=F32) + bias_ref[0, r - r0]
        s_cx = lax.dot_general(qs, kc, NT_DIMS, preferred_element_type=F32)
        m = jnp.max(jnp.maximum(_fold_lanes(s_nb, jnp.maximum), _fold_lanes(s_cx, jnp.maximum)),
                    axis=-1, keepdims=True)
        p_nb = jnp.exp(s_nb - m)
        p_cx = jnp.exp(s_cx - m)
        denom = jnp.sum(_fold_lanes(p_nb, jnp.add) + _fold_lanes(p_cx, jnp.add), axis=-1, keepdims=True)
        o2 = (jnp.dot(p_nb.astype(BF16), vspan, preferred_element_type=F32)
              + jnp.dot(p_cx.astype(BF16), vc, preferred_element_type=F32)) / denom
        o_ref[0, pl.ds(q_off, GRID_W), :] = _merge_heads(o2).astype(BF16)
        return carry

    lax.fori_loop(0, rows, row_body, 0, unroll=4)


def _na_bias_tables(rpb):
    h = rpb.shape[0]
    w, kw = GRID_W, NA_KW
    cidx = jnp.arange(w)
    c0 = jnp.clip(cidx - kw // 2, 0, w - kw)
    in_win = (cidx[None, :] >= c0[:, None]) & (cidx[None, :] < c0[:, None] + kw)
    edge = w - kw
    ext = jnp.concatenate([jnp.broadcast_to(rpb[..., :1], rpb.shape[:-1] + (edge,)), rpb.astype(F32),
                           jnp.broadcast_to(rpb[..., -1:], rpb.shape[:-1] + (edge,))], axis=-1)
    ext = jnp.concatenate([ext, jnp.zeros_like(ext[..., :1])], axis=-1)
    flow = jnp.tile(ext, (1, 1, w))[..., :w * (2 * w - 1)].reshape(ext.shape[:-1] + (w, 2 * w - 1))
    toep = flow[..., w - 1:]
    toep = jnp.where(in_win, toep, NEG_INF)
    tab = jnp.stack([toep[:, NA_KH - 1 - o:2 * NA_KH - 1 - o] for o in range(NA_KH)], axis=1)
    g = NA_GROUP
    tab = tab.reshape(h // g, g, NA_KH, NA_KH, w, w).transpose(0, 2, 1, 4, 3, 5)
    return tab.reshape(h // g, NA_KH, g * w, NA_KH * w)


def _na_attention(qkv, qkv_ctx, bias):
    b, t, _ = qkv.shape
    l = qkv_ctx.shape[1]
    rows = t // GRID_W
    assert rows >= NA_KH
    groups = NA_HEADS // NA_GROUP
    lanes = NA_GROUP * NA_HEAD_DIM
    return pl.pallas_call(
        functools.partial(_na_kernel, rows=rows),
        out_shape=jax.ShapeDtypeStruct((b, t, NA_WIDTH), BF16),
        grid=(b, groups),
        in_specs=[pl.BlockSpec((1, t, lanes), lambda bi, hg: (bi, 0, hg)),
                  pl.BlockSpec((1, t, lanes), lambda bi, hg: (bi, 0, groups + hg)),
                  pl.BlockSpec((1, t, lanes), lambda bi, hg: (bi, 0, 2 * groups + hg)),
                  pl.BlockSpec((1, l, lanes), lambda bi, hg: (bi, 0, groups + hg)),
                  pl.BlockSpec((1, l, lanes), lambda bi, hg: (bi, 0, 2 * groups + hg)),
                  pl.BlockSpec((1, NA_KH, NA_GROUP * GRID_W, NA_KH * GRID_W), lambda bi, hg: (hg, 0, 0, 0))],
        out_specs=pl.BlockSpec((1, t, lanes), lambda bi, hg: (bi, 0, hg)),
        compiler_params=_params(2),
        name="na_attention",
    )(qkv, qkv, qkv, qkv_ctx, qkv_ctx, bias)


def _ctx_attn_kernel(q_ref, k_ref, v_ref, o_ref):
    qs = _split_heads(q_ref[0])
    s = lax.dot_general(qs, k_ref[0], NT_DIMS, preferred_element_type=F32)
    p = jnp.exp(s - jnp.max(s, axis=-1, keepdims=True))
    denom = jnp.sum(p, axis=-1, keepdims=True)
    o2 = jnp.dot(p.astype(BF16), v_ref[0], preferred_element_type=F32) / denom
    o_ref[0] = _merge_heads(o2).astype(BF16)


def _ctx_attention(qkv_ctx):
    b, l, _ = qkv_ctx.shape
    groups = NA_HEADS // NA_GROUP
    lanes = NA_GROUP * NA_HEAD_DIM
    return pl.pallas_call(
        _ctx_attn_kernel,
        out_shape=jax.ShapeDtypeStruct((b, l, NA_WIDTH), BF16),
        grid=(b, groups),
        in_specs=[pl.BlockSpec((1, l, lanes), lambda bi, hg: (bi, 0, hg)),
                  pl.BlockSpec((1, l, lanes), lambda bi, hg: (bi, 0, groups + hg)),
                  pl.BlockSpec((1, l, lanes), lambda bi, hg: (bi, 0, 2 * groups + hg))],
        out_specs=pl.BlockSpec((1, l, lanes), lambda bi, hg: (bi, 0, hg)),
        compiler_params=_params(2),
        name="ctx_attention",
    )(qkv_ctx, qkv_ctx, qkv_ctx)


def _hyb_out_kernel(a_ref, u_ref, vn_ref, ws_ref, bs_ref, w_ref, x_ref, g1_ref, ng_ref, o_ref, s_ref):
    tm = a_ref.shape[1]
    c = SGU_CHUNK
    gd = vn_ref.shape[2] // SGU_GROUPS
    for ci in range(tm // c):
        rows = slice(ci * c, (ci + 1) * c)
        for gp in range(SGU_GROUPS // 2):
            lanes = slice(gp * 2 * gd, (gp + 1) * 2 * gd)
            mixed2 = jnp.dot(ws_ref[gp], vn_ref[0, rows, lanes], preferred_element_type=F32)
            lane = lax.broadcasted_iota(jnp.int32, (c, 2 * gd), 1)
            mixed = jnp.where(lane < gd, mixed2[:c], mixed2[c:]) + bs_ref[:, lanes]
            s_ref[rows, lanes] = (u_ref[0, rows, lanes].astype(F32) * mixed).astype(BF16)
    wa = a_ref.shape[2]
    y = (jnp.dot(a_ref[0], w_ref[:wa, :], preferred_element_type=F32)
         + jnp.dot(s_ref[...], w_ref[wa:, :], preferred_element_type=F32))
    o_ref[0] = x_ref[0] + g1_ref[0] * _rms(y, ng_ref[...])


def _hyb_out(a, u, vn, ws_pairs, bs_tab, w_bf, x, gate, gain):
    b, n, d = x.shape
    tm = _token_tile(n, 512)
    w = a.shape[2]
    row = lambda bi, i: (bi, 0, 0)
    tile = lambda bi, i: (bi, i, 0)
    return pl.pallas_call(
        _hyb_out_kernel,
        out_shape=jax.ShapeDtypeStruct((b, n, d), F32),
        grid=(b, n // tm),
        in_specs=[pl.BlockSpec((1, tm, w), tile),
                  pl.BlockSpec((1, tm, w), tile),
                  pl.BlockSpec((1, tm, w), tile),
                  _resident(ws_pairs.shape),
                  _resident(bs_tab.shape),
                  _resident(w_bf.shape),
                  pl.BlockSpec((1, tm, d), tile),
                  pl.BlockSpec((1, 1, d), row),
                  _resident((1, d))],
        out_specs=pl.BlockSpec((1, tm, d), tile),
        scratch_shapes=[pltpu.VMEM((tm, w), BF16)],
        compiler_params=_params(2),
        name="hyb_out",
    )(a, u, vn, ws_pairs, bs_tab, w_bf, x, gate, gain)


def _ret_in_kernel(x_ref, sh_ref, sc_ref, g_ref, w_ref, cos_ref, sin_ref, o_ref, *, rope, dk):
    hb = _norm_mod(x_ref[0], g_ref[...], sh_ref[0], sc_ref[0]).astype(BF16)
    qk_w = RET_HEADS * dk
    n_out = o_ref.shape[2]
    step = 512
    for c0 in range(0, n_out, step):
        acc = jnp.dot(hb, w_ref[:, c0:c0 + step], preferred_element_type=F32)
        if c0 < 2 * qk_w:
            if c0 >= qk_w:
                acc = acc * (dk ** -0.5)
            if rope:
                parts = []
                for hd in range(step // dk):
                    t = acc[:, hd * dk:(hd + 1) * dk]
                    parts.append(t * cos_ref[...] + pltpu.roll(t, dk // 2, 1) * sin_ref[...])
                acc = jnp.concatenate(parts, axis=-1)
        elif c0 >= 2 * qk_w + (n_out - 2 * qk_w) // 2:
            acc = _silu(acc)
        o_ref[0, :, c0:c0 + step] = acc.astype(BF16)


def _ret_in(x, shift, scale, gain, w_bf, cos2, sin2, rope):
    b, n, d = x.shape
    tm = _token_tile(n, 512)
    n_out = w_bf.shape[1]
    dk = cos2.shape[1]
    row = lambda bi, i: (bi, 0, 0)
    tile = lambda bi, i: (bi, i, 0)
    return pl.pallas_call(
        functools.partial(_ret_in_kernel, rope=rope, dk=dk),
        out_shape=jax.ShapeDtypeStruct((b, n, n_out), BF16),
        grid=(b, n // tm),
        in_specs=[pl.BlockSpec((1, tm, d), tile),
                  pl.BlockSpec((1, 1, d), row),
                  pl.BlockSpec((1, 1, d), row),
                  _resident((1, d)),
                  _resident(w_bf.shape),
                  pl.BlockSpec((tm, dk), lambda bi, i: (i, 0)),
                  pl.BlockSpec((tm, dk), lambda bi, i: (i, 0))],
        out_specs=pl.BlockSpec((1, tm, n_out), tile),
        compiler_params=_params(2),
        name="ret_in",
    )(x, shift, scale, gain, w_bf, cos2, sin2)


def _ret_scan_kernel(lg_ref, q_ref, k_ref, v_ref, s0f_ref, s0b_ref, on_ref, sf_ref, sb_ref, pre_ref):
    head = pl.program_id(1)
    n, dv = v_ref.shape[1], v_ref.shape[2]
    c = RET_CHUNK
    nc = n // c
    lgf = lg_ref[0, head]
    lgb = lg_ref[1, head]
    pos = lax.broadcasted_iota(jnp.int32, (c, 1), 0).astype(F32)
    diff = (lax.broadcasted_iota(jnp.int32, (c, c), 0) - lax.broadcasted_iota(jnp.int32, (c, c), 1)).astype(F32)
    lower, upper = diff >= 0, diff <= 0
    decay = (jnp.where(lower, jnp.exp(lgf * jnp.where(lower, diff, 0.0)), 0.0)
             + jnp.where(upper, jnp.exp(lgb * jnp.where(upper, -diff, 0.0)), 0.0))
    dq_f, dk_f, dc_f = jnp.exp(lgf * (pos + 1.0)), jnp.exp(lgf * (c - 1.0 - pos)), jnp.exp(lgf * c)
    dq_b, dk_b, dc_b = jnp.exp(lgb * (c - pos)), jnp.exp(lgb * pos), jnp.exp(lgb * c)

    sf_ref[0, 0] = s0f_ref[0, 0]
    sb_ref[0, 0] = s0b_ref[0, 0]

    def advance(ci, st_ref, lanes, dk_row, dchunk):
        off = pl.multiple_of(ci * c, c)
        kd = (k_ref[0, pl.ds(off, c), :].astype(F32) * dk_row).astype(BF16)
        upd = lax.dot_general(kd, v_ref[0, pl.ds(off, c), :], TN_DIMS, preferred_element_type=F32)
        state = st_ref[0, 0]
        pre_ref[ci, :, lanes] = state.astype(BF16)
        st_ref[0, 0] = state * dchunk + upd

    def state_body(i, carry):
        advance(i, sf_ref, slice(0, dv), dk_f, dc_f)
        advance(nc - 1 - i, sb_ref, slice(dv, 2 * dv), dk_b, dc_b)
        return carry

    lax.fori_loop(0, nc, state_body, 0, unroll=min(8, nc))

    def out_body(ci, carry):
        off = pl.multiple_of(ci * c, c)
        q = q_ref[0, pl.ds(off, c), :]
        k = k_ref[0, pl.ds(off, c), :]
        v = v_ref[0, pl.ds(off, c), :]
        s = lax.dot_general(q, k, NT_DIMS, preferred_element_type=F32) * decay
        cross = jnp.dot(q, pre_ref[ci], preferred_element_type=F32)
        tot = (jnp.dot(s.astype(BF16), v, preferred_element_type=F32)
               + dq_f * cross[:, :dv] + dq_b * cross[:, dv:])
        mu = jnp.mean(tot, axis=-1, keepdims=True)
        tc = tot - mu
        var = jnp.mean(tc * tc, axis=-1, keepdims=True)
        on_ref[0, pl.ds(off, c), :] = (tc * lax.rsqrt(var + EPS)).astype(BF16)
        return carry

    lax.fori_loop(0, nc, out_body, 0, unroll=min(8, nc))


def _ret_scan(qkvg, log_decay, s0f, s0b, dk, dv):
    b, n, _ = qkvg.shape
    h = RET_HEADS
    v_blk0 = (2 * h * dk) // dv
    state = jax.ShapeDtypeStruct((b, h, dk, dv), F32)
    st_spec = pl.BlockSpec((1, 1, dk, dv), lambda bi, hi: (bi, hi, 0, 0))
    return pl.pallas_call(
        _ret_scan_kernel,
        out_shape=(jax.ShapeDtypeStruct((b, n, h * dv), BF16), state, state),
        grid=(b, h),
        in_specs=[pl.BlockSpec(memory_space=pltpu.SMEM),
                  pl.BlockSpec((1, n, dk), lambda bi, hi: (bi, 0, hi)),
                  pl.BlockSpec((1, n, dk), lambda bi, hi: (bi, 0, h + hi)),
                  pl.BlockSpec((1, n, dv), lambda bi, hi: (bi, 0, v_blk0 + hi)),
                  st_spec, st_spec],
        out_specs=(pl.BlockSpec((1, n, dv), lambda bi, hi: (bi, 0, hi)), st_spec, st_spec),
        scratch_shapes=[pltpu.VMEM((n // RET_CHUNK, dk, 2 * dv), BF16)],
        compiler_params=_params(2),
        name="ret_scan",
    )(log_decay, qkvg, qkvg, qkvg, s0f, s0b)


def _ret_out_kernel(gt_ref, on_ref, gn_ref, w_ref, x_ref, g1_ref, ng_ref, o_ref):
    z = (gt_ref[0].astype(F32) * (on_ref[0].astype(F32) * gn_ref[...])).astype(BF16)
    y = jnp.dot(z, w_ref[...], preferred_element_type=F32)
    o_ref[0] = x_ref[0] + g1_ref[0] * _rms(y, ng_ref[...])


def _ret_out(qkvg, on, gn_g, w_bf, x, gate, gain):
    b, n, d = x.shape
    tm = _token_tile(n, 512)
    vw = on.shape[2]
    g_blk = qkvg.shape[2] // vw - 1
    row = lambda bi, i: (bi, 0, 0)
    tile = lambda bi, i: (bi, i, 0)
    return pl.pallas_call(
        _ret_out_kernel,
        out_shape=jax.ShapeDtypeStruct((b, n, d), F32),
        grid=(b, n // tm),
        in_specs=[pl.BlockSpec((1, tm, vw), lambda bi, i: (bi, i, g_blk)),
                  pl.BlockSpec((1, tm, vw), tile),
                  _resident((1, vw)),
                  _resident(w_bf.shape),
                  pl.BlockSpec((1, tm, d), tile),
                  pl.BlockSpec((1, 1, d), row),
                  _resident((1, d))],
        out_specs=pl.BlockSpec((1, tm, d), tile),
        compiler_params=_params(2),
        name="ret_out",
    )(qkvg, on, gn_g, w_bf, x, gate, gain)


def _ffn_kernel(xp_ref, x_ref, xn_ref, sh_ref, sc_ref, g_ref, win_ref, cw_ref, cb_ref, wout_ref, g2_ref, ng_ref,
                o_ref, h_ref, a0_ref, a1_ref, acc_ref):
    i = pl.program_id(1)
    last = pl.num_programs(1) - 1
    tm = x_ref.shape[1]
    gain, shift, scale = g_ref[...], sh_ref[0], sc_ref[0]
    hp = jnp.where(i > 0, _norm_mod(xp_ref[0], gain, shift, scale), 0.0)
    hn = jnp.where(i < last, _norm_mod(xn_ref[0], gain, shift, scale), 0.0)
    h_ref[:HALO] = hp.astype(BF16)
    h_ref[HALO:HALO + tm] = _norm_mod(x_ref[0], gain, shift, scale).astype(BF16)
    h_ref[HALO + tm:] = hn.astype(BF16)
    fc = FFN_CHUNK
    ff = wout_ref.shape[0]
    nch = ff // fc

    def cols(ci, half):
        return pl.ds(pl.multiple_of(half * ff + ci * fc, fc), fc)

    def project(ci, a_ref):
        for half in range(2):
            a_ref[:, half * fc:(half + 1) * fc] = jnp.dot(h_ref[...], win_ref[:, cols(ci, half)],
                                                          preferred_element_type=F32)

    def conv(ci, a_ref, half):
        lanes = slice(half * fc, (half + 1) * fc)
        cw = cw_ref[:, cols(ci, half)]
        return (a_ref[pl.ds(HALO - 1, tm), lanes] * cw[0:1] + a_ref[pl.ds(HALO, tm), lanes] * cw[1:2]
                + a_ref[pl.ds(HALO + 1, tm), lanes] * cw[2:3] + cb_ref[:, cols(ci, half)])

    def mix(ci, a_ref):
        act = (_silu(conv(ci, a_ref, 0)) * conv(ci, a_ref, 1)).astype(BF16)
        acc_ref[...] += jnp.dot(act, wout_ref[pl.ds(pl.multiple_of(ci * fc, fc), fc), :],
                                preferred_element_type=F32)

    acc_ref[...] = jnp.zeros_like(acc_ref)
    project(0, a0_ref)

    def pair_body(j, carry):
        project(2 * j + 1, a1_ref)
        mix(2 * j, a0_ref)
        project(2 * j + 2, a0_ref)
        mix(2 * j + 1, a1_ref)
        return carry

    assert nch % 2 == 1
    lax.fori_loop(0, nch // 2, pair_body, 0)
    mix(nch - 1, a0_ref)
    o_ref[0] = x_ref[0] + g2_ref[0] * _rms(acc_ref[...], ng_ref[...])


def _conv_ffn(x, shift, scale, gain, win_bf, conv_w, conv_b, wout_bf, gate, gain_out):
    b, n, d = x.shape
    tm = _token_tile(n, 1024)
    assert tm % HALO == 0
    ff = wout_bf.shape[0]
    assert ff % FFN_CHUNK == 0 and (ff // FFN_CHUNK) % 2 == 1
    hb = tm // HALO
    n_hb = n // HALO
    row = lambda bi, i: (bi, 0, 0)
    tile = lambda bi, i: (bi, i, 0)
    return pl.pallas_call(
        _ffn_kernel,
        out_shape=jax.ShapeDtypeStruct((b, n, d), F32),
        grid=(b, n // tm),
        in_specs=[pl.BlockSpec((1, HALO, d), lambda bi, i: (bi, jnp.maximum(i * hb - 1, 0), 0)),
                  pl.BlockSpec((1, tm, d), tile),
                  pl.BlockSpec((1, HALO, d), lambda bi, i: (bi, jnp.minimum((i + 1) * hb, n_hb - 1), 0)),
                  pl.BlockSpec((1, 1, d), row),
                  pl.BlockSpec((1, 1, d), row),
                  _resident((1, d)),
                  _resident(win_bf.shape),
                  _resident(conv_w.shape),
                  _resident(conv_b.shape),
                  _resident(wout_bf.shape),
                  pl.BlockSpec((1, 1, d), row),
                  _resident((1, d))],
        out_specs=pl.BlockSpec((1, tm, d), tile),
        scratch_shapes=[pltpu.VMEM((tm + 2 * HALO, d), BF16),
                        pltpu.VMEM((tm + 2 * HALO, 2 * FFN_CHUNK), F32),
                        pltpu.VMEM((tm + 2 * HALO, 2 * FFN_CHUNK), F32),
                        pltpu.VMEM((tm, d), F32)],
        compiler_params=_params(2),
        name="conv_ffn",
    )(x, x, x, shift, scale, gain, win_bf, conv_w, conv_b, wout_bf, gate, gain_out)


def _rope_tables(t, dk):
    tok = jnp.arange(t)
    row = (tok // GRID_W).astype(F32)
    col = (tok % GRID_W).astype(F32)
    n_freq = dk // 4
    inv = ROPE_BASE ** (-jnp.arange(n_freq, dtype=F32) / n_freq)
    ang = jnp.concatenate([row[:, None] * inv, col[:, None] * inv], axis=-1)
    cos, sin = jnp.cos(ang), jnp.sin(ang)
    return jnp.concatenate([cos, cos], axis=-1), jnp.concatenate([-sin, sin], axis=-1)


def kernel(x, c, ctx, c_ctx, ada_w, ada_b, norm_g, hyb_w_in, na_rpb, sgu_w, sgu_b, hyb_w_out, ret_w_in, ret_log_decay,
           ret_gn_g, ret_w_out, ffn_w_in, ffn_conv_w, ffn_conv_b, ffn_w_out):
    b, t, d = x.shape
    depth = ada_w.shape[0]
    dk = d // RET_HEADS
    dv = 2 * dk

    cond = jnp.zeros((8, d), F32).at[:b].set(c).at[b].set(c_ctx)
    mods = _ada_rows(cond, ada_w, ada_b)
    cos2, sin2 = _rope_tables(t, dk)
    xc = ctx

    for i in range(depth):
        ctx_out = i < depth - 1
        j = i // 2
        lat = [mods[i, :b, m * d:(m + 1) * d][:, None, :] for m in range(6)]
        cxm = [jnp.broadcast_to(mods[i, b, m * d:(m + 1) * d][None, None, :], (b, 1, d)) for m in range(6)]
        gains = [norm_g[i, m][None, :] for m in range(4)]

        if i % 2 == 0:
            w_in = hyb_w_in[j].astype(BF16)
            w_out = hyb_w_out[j].astype(BF16)
            ws_pairs = sgu_w[j].astype(BF16).reshape(SGU_GROUPS // 2, 2 * SGU_CHUNK, SGU_CHUNK)
            bs_tab = jnp.repeat(sgu_b[j].T, (d - NA_WIDTH) // SGU_GROUPS, axis=1)
            bias = _na_bias_tables(na_rpb[j])
            qkv_c, u_c, vn_c = _hyb_in(xc, cxm[0], cxm[1], gains[0], w_in)
            qkv_l, u_l, vn_l = _hyb_in(x, lat[0], lat[1], gains[0], w_in)
            a_l = _na_attention(qkv_l, qkv_c, bias)
            x = _hyb_out(a_l, u_l, vn_l, ws_pairs, bs_tab, w_out, x, lat[2], gains[1])
            if ctx_out:
                a_c = _ctx_attention(qkv_c)
                xc = _hyb_out(a_c, u_c, vn_c, ws_pairs, bs_tab, w_out, xc, cxm[2], gains[1])
        else:
            w_in = ret_w_in[j].astype(BF16)
            w_out = ret_w_out[j].astype(BF16)
            lg = ret_log_decay[j].astype(F32)
            gn = ret_gn_g[j][None, :]
            zero_state = jnp.zeros((b, RET_HEADS, dk, dv), F32)
            p_c = _ret_in(xc, cxm[0], cxm[1], gains[0], w_in, cos2, sin2, rope=False)
            p_l = _ret_in(x, lat[0], lat[1], gains[0], w_in, cos2, sin2, rope=True)
            on_c, sf, sb = _ret_scan(p_c, lg, zero_state, zero_state, dk, dv)
            on_l, _, _ = _ret_scan(p_l, lg, sf, sb, dk, dv)
            x = _ret_out(p_l, on_l, gn, w_out, x, lat[2], gains[1])
            if ctx_out:
                xc = _ret_out(p_c, on_c, gn, w_out, xc, cxm[2], gains[1])

        ffn_w = (ffn_w_in[i].astype(BF16), ffn_conv_w[i], ffn_conv_b[i][None, :], ffn_w_out[i].astype(BF16))
        x = _conv_ffn(x, lat[3], lat[4], gains[2], *ffn_w, lat[5], gains[3])
        if ctx_out:
            xc = _conv_ffn(xc, cxm[3], cxm[4], gains[2], *ffn_w, cxm[5], gains[3])
    return x
```

```python
import functools

import jax
import jax.numpy as jnp
from jax import lax
from jax.experimental import pallas as pl
from jax.experimental.pallas import tpu as pltpu

F32 = jnp.float32
BF16 = jnp.bfloat16

GRID_W = 64
EPS = 1e-6
NEG_INF = -1e30
LOG2E = 1.4426950408889634

NA_HEADS = 8
NA_HEAD_DIM = 64
NA_WIDTH = NA_HEADS * NA_HEAD_DIM
NA_KH = 8
NA_KW = 16
NA_GROUP = 4
SGU_GROUPS = 8
SGU_CHUNK = 128

RET_HEADS = 8
RET_CHUNK = 256
ROPE_BASE = 10000.0

CONV_W = 3
FFN_CHUNK = 256
HALO = 16

V7X_VMEM_LIMIT_BYTES = 56 * 1024 * 1024

NT_DIMS = (((1,), (1,)), ((), ()))
TN_DIMS = (((0,), (0,)), ((), ()))


def _params(n_axes):
    return pltpu.CompilerParams(dimension_semantics=("arbitrary",) * n_axes,
                                vmem_limit_bytes=V7X_VMEM_LIMIT_BYTES)


def _resident(shape):
    zeros = (0,) * len(shape)
    return pl.BlockSpec(shape, lambda *_: zeros, pipeline_mode=pl.Buffered(1))


def _token_tile(n, want):
    tm = min(n, want)
    assert n % tm == 0
    return tm


def _rms(v, gain):
    return v * lax.rsqrt(jnp.mean(v * v, axis=-1, keepdims=True) + EPS) * gain


def _silu(v):
    return v * (1.0 / (1.0 + jnp.exp(-v)))


def _norm_mod(x, gain, shift, scale):
    return _rms(x, gain) * (1.0 + scale) + shift


def _ada_kernel(c_ref, w_ref, b_ref, o_ref):
    s = _silu(c_ref[...])
    o_ref[0] = jnp.dot(s, w_ref[0], preferred_element_type=F32) + b_ref[0]


def _ada_rows(cond, ada_w, ada_b):
    depth, d, n = ada_w.shape
    tn = 2048
    return pl.pallas_call(
        _ada_kernel,
        out_shape=jax.ShapeDtypeStruct((depth, 8, n), F32),
        grid=(depth, n // tn),
        in_specs=[pl.BlockSpec((8, d), lambda i, j: (0, 0)),
                  pl.BlockSpec((1, d, tn), lambda i, j: (i, 0, j)),
                  pl.BlockSpec((1, 1, tn), lambda i, j: (i, 0, j))],
        out_specs=pl.BlockSpec((1, 8, tn), lambda i, j: (i, 0, j)),
        compiler_params=_params(2),
        name="ada_rows",
    )(cond, ada_w, ada_b.reshape(depth, 1, n))


def _hyb_in_kernel(x_ref, sh_ref, sc_ref, g_ref, w_ref, qkv_ref, u_ref, vn_ref):
    hb = _norm_mod(x_ref[0], g_ref[...], sh_ref[0], sc_ref[0]).astype(BF16)
    w = NA_WIDTH
    for part in range(3):
        acc = jnp.dot(hb, w_ref[:, part * w:(part + 1) * w], preferred_element_type=F32)
        if part == 0:
            acc = acc * (NA_HEAD_DIM ** -0.5 * LOG2E)
        qkv_ref[0, :, part * w:(part + 1) * w] = acc.astype(BF16)
    u = jax.nn.gelu(jnp.dot(hb, w_ref[:, 3 * w:4 * w], preferred_element_type=F32))
    u_ref[0] = u.astype(BF16)
    g = jax.nn.gelu(jnp.dot(hb, w_ref[:, 4 * w:5 * w], preferred_element_type=F32))
    mu = jnp.mean(g, axis=-1, keepdims=True)
    gc = g - mu
    var = jnp.mean(gc * gc, axis=-1, keepdims=True)
    vn_ref[0] = (gc * lax.rsqrt(var + EPS)).astype(BF16)


def _hyb_in(x, shift, scale, gain, w_bf):
    b, n, d = x.shape
    tm = _token_tile(n, 512)
    w = NA_WIDTH
    row = lambda bi, i: (bi, 0, 0)
    tile = lambda bi, i: (bi, i, 0)
    return pl.pallas_call(
        _hyb_in_kernel,
        out_shape=(jax.ShapeDtypeStruct((b, n, 3 * w), BF16),
                   jax.ShapeDtypeStruct((b, n, w), BF16),
                   jax.ShapeDtypeStruct((b, n, w), BF16)),
        grid=(b, n // tm),
        in_specs=[pl.BlockSpec((1, tm, d), tile),
                  pl.BlockSpec((1, 1, d), row),
                  pl.BlockSpec((1, 1, d), row),
                  _resident((1, d)),
                  _resident(w_bf.shape)],
        out_specs=(pl.BlockSpec((1, tm, 3 * w), tile),
                   pl.BlockSpec((1, tm, w), tile),
                   pl.BlockSpec((1, tm, w), tile)),
        compiler_params=_params(2),
        name="hyb_in",
    )(x, shift, scale, gain, w_bf)


def _split_heads(qg):
    lane = lax.broadcasted_iota(jnp.int32, qg.shape, 1)
    zero = jnp.zeros_like(qg)
    return jnp.concatenate([jnp.where(lane // NA_HEAD_DIM == hd, qg, zero) for hd in range(NA_GROUP)], axis=0)


def _merge_heads(og):
    m = og.shape[0] // NA_GROUP
    lane = lax.broadcasted_iota(jnp.int32, (m, og.shape[1]), 1)
    out = og[:m]
    for hd in range(1, NA_GROUP):
        out = jnp.where(lane // NA_HEAD_DIM == hd, og[hd * m:(hd + 1) * m], out)
    return out


def _fold_lanes(v, op, width=128):
    out = v[:, :width]
    for c0 in range(width, v.shape[1], width):
        out = op(out, v[:, c0:c0 + width])
    return out


def _na_kernel(q_ref, k_ref, v_ref, kc_ref, vc_ref, bias_ref, o_ref, snb0_ref, scx0_ref, snb1_ref, scx1_ref, *, rows):
    kh = NA_KH
    span = kh * GRID_W
    bufs = ((snb0_ref, scx0_ref), (snb1_ref, scx1_ref))

    def window(r):
        r0 = jnp.clip(r - kh // 2, 0, rows - kh)
        return r0, pl.multiple_of(r * GRID_W, GRID_W), pl.multiple_of(r0 * GRID_W, GRID_W)

    def scores(r, buf):
        snb_ref, scx_ref = buf
        r0, q_off, k_off = window(r)
        qs = _split_heads(q_ref[0, pl.ds(q_off, GRID_W), :])
        s = lax.dot_general(qs, k_ref[0, pl.ds(k_off, span), :], NT_DIMS, preferred_element_type=F32)
        dr0 = kh - 1 - (r - r0)
        base = jnp.where(dr0 % 2 == 0, dr0 // 2, kh + (dr0 - 1) // 2)
        for jj in range(kh // 2):
            lanes = slice(jj * 2 * GRID_W, (jj + 1) * 2 * GRID_W)
            snb_ref[:, lanes] = s[:, lanes] + bias_ref[0, base + jj]
        scx_ref[...] = lax.dot_general(qs, kc_ref[0], NT_DIMS, preferred_element_type=F32)

    def attend(r, buf):
        snb_ref, scx_ref = buf
        _, q_off, k_off = window(r)
        s_nb = snb_ref[...]
        s_cx = scx_ref[...]
        m = jnp.max(jnp.maximum(_fold_lanes(s_nb, jnp.maximum), _fold_lanes(s_cx, jnp.maximum)),
                    axis=-1, keepdims=True)
        p_nb = jnp.exp2(s_nb - m)
        p_cx = jnp.exp2(s_cx - m)
        denom = jnp.sum(_fold_lanes(p_nb, jnp.add) + _fold_lanes(p_cx, jnp.add), axis=-1, keepdims=True)
        o2 = (jnp.dot(p_nb.astype(BF16), v_ref[0, pl.ds(k_off, span), :], preferred_element_type=F32)
              + jnp.dot(p_cx.astype(BF16), vc_ref[0], preferred_element_type=F32)) / denom
        o_ref[0, pl.ds(q_off, GRID_W), :] = _merge_heads(o2).astype(BF16)

    scores(0, bufs[0])

    def pair_body(j, carry):
        r = 2 * j
        scores(r + 1, bufs[1])
        attend(r, bufs[0])
        scores(jnp.minimum(r + 2, rows - 1), bufs[0])
        attend(r + 1, bufs[1])
        return carry

    assert rows % 2 == 0
    lax.fori_loop(0, rows // 2, pair_body, 0, unroll=2)


def _na_bias_tables(rpb):
    h = rpb.shape[0]
    w, kw = GRID_W, NA_KW
    cidx = jnp.arange(w)
    c0 = jnp.clip(cidx - kw // 2, 0, w - kw)
    in_win = (cidx[None, :] >= c0[:, None]) & (cidx[None, :] < c0[:, None] + kw)
    edge = w - kw
    ext = jnp.concatenate([jnp.broadcast_to(rpb[..., :1], rpb.shape[:-1] + (edge,)), rpb.astype(F32),
                           jnp.broadcast_to(rpb[..., -1:], rpb.shape[:-1] + (edge,))], axis=-1)
    ext = jnp.concatenate([ext, jnp.zeros_like(ext[..., :1])], axis=-1)
    flow = jnp.tile(ext, (1, 1, w))[..., :w * (2 * w - 1)].reshape(ext.shape[:-1] + (w, 2 * w - 1))
    toep = flow[..., w - 1:]
    toep = jnp.where(in_win, toep, NEG_INF) * LOG2E
    g = NA_GROUP
    n_dr = toep.shape[1]
    tg = toep.reshape(h // g, g, n_dr, w, w).transpose(0, 2, 1, 3, 4).reshape(h // g, n_dr, g * w, w)
    tg = jnp.concatenate([tg, jnp.zeros_like(tg[:, :1])], axis=1)
    even = jnp.concatenate([tg[:, 0::2], tg[:, 1::2]], axis=-1)
    odd = jnp.concatenate([tg[:, 1:-1:2], tg[:, 2::2]], axis=-1)
    return jnp.concatenate([even, odd], axis=1)


def _na_attention(qkv, qkv_ctx, bias):
    b, t, _ = qkv.shape
    l = qkv_ctx.shape[1]
    rows = t // GRID_W
    assert rows >= NA_KH
    groups = NA_HEADS // NA_GROUP
    lanes = NA_GROUP * NA_HEAD_DIM
    return pl.pallas_call(
        functools.partial(_na_kernel, rows=rows),
        out_shape=jax.ShapeDtypeStruct((b, t, NA_WIDTH), BF16),
        grid=(b, groups),
        in_specs=[pl.BlockSpec((1, t, lanes), lambda bi, hg: (bi, 0, hg)),
                  pl.BlockSpec((1, t, lanes), lambda bi, hg: (bi, 0, groups + hg)),
                  pl.BlockSpec((1, t, lanes), lambda bi, hg: (bi, 0, 2 * groups + hg)),
                  pl.BlockSpec((1, l, lanes), lambda bi, hg: (bi, 0, groups + hg)),
                  pl.BlockSpec((1, l, lanes), lambda bi, hg: (bi, 0, 2 * groups + hg)),
                  pl.BlockSpec((1,) + bias.shape[1:], lambda bi, hg: (hg, 0, 0, 0))],
        out_specs=pl.BlockSpec((1, t, lanes), lambda bi, hg: (bi, 0, hg)),
        scratch_shapes=[pltpu.VMEM((NA_GROUP * GRID_W, NA_KH * GRID_W), F32),
                        pltpu.VMEM((NA_GROUP * GRID_W, l), F32)] * 2,
        compiler_params=_params(2),
        name="na_attention",
    )(qkv, qkv, qkv, qkv_ctx, qkv_ctx, bias)


def _ctx_attn_kernel(q_ref, k_ref, v_ref, o_ref):
    qs = _split_heads(q_ref[0])
    s = lax.dot_general(qs, k_ref[0], NT_DIMS, preferred_element_type=F32)
    p = jnp.exp2(s - jnp.max(s, axis=-1, keepdims=True))
    denom = jnp.sum(p, axis=-1, keepdims=True)
    o2 = jnp.dot(p.astype(BF16), v_ref[0], preferred_element_type=F32) / denom
    o_ref[0] = _merge_heads(o2).astype(BF16)


def _ctx_attention(qkv_ctx):
    b, l, _ = qkv_ctx.shape
    groups = NA_HEADS // NA_GROUP
    lanes = NA_GROUP * NA_HEAD_DIM
    return pl.pallas_call(
        _ctx_attn_kernel,
        out_shape=jax.ShapeDtypeStruct((b, l, NA_WIDTH), BF16),
        grid=(b, groups),
        in_specs=[pl.BlockSpec((1, l, lanes), lambda bi, hg: (bi, 0, hg)),
                  pl.BlockSpec((1, l, lanes), lambda bi, hg: (bi, 0, groups + hg)),
                  pl.BlockSpec((1, l, lanes), lambda bi, hg: (bi, 0, 2 * groups + hg))],
        out_specs=pl.BlockSpec((1, l, lanes), lambda bi, hg: (bi, 0, hg)),
        compiler_params=_params(2),
        name="ctx_attention",
    )(qkv_ctx, qkv_ctx, qkv_ctx)


def _hyb_out_kernel(a_ref, u_ref, vn_ref, ws_ref, bs_ref, w_ref, x_ref, g1_ref, ng_ref, o_ref, s_ref):
    tm = a_ref.shape[1]
    c = SGU_CHUNK
    gd = vn_ref.shape[2] // SGU_GROUPS
    for ci in range(tm // c):
        rows = slice(ci * c, (ci + 1) * c)
        for gp in range(SGU_GROUPS // 2):
            lanes = slice(gp * 2 * gd, (gp + 1) * 2 * gd)
            mixed2 = jnp.dot(ws_ref[gp], vn_ref[0, rows, lanes], preferred_element_type=F32)
            lane = lax.broadcasted_iota(jnp.int32, (c, 2 * gd), 1)
            mixed = jnp.where(lane < gd, mixed2[:c], mixed2[c:]) + bs_ref[:, lanes]
            s_ref[rows, lanes] = (u_ref[0, rows, lanes].astype(F32) * mixed).astype(BF16)
    wa = a_ref.shape[2]
    y = (jnp.dot(a_ref[0], w_ref[:wa, :], preferred_element_type=F32)
         + jnp.dot(s_ref[...], w_ref[wa:, :], preferred_element_type=F32))
    o_ref[0] = x_ref[0] + g1_ref[0] * _rms(y, ng_ref[...])


def _hyb_out(a, u, vn, ws_pairs, bs_tab, w_bf, x, gate, gain):
    b, n, d = x.shape
    tm = _token_tile(n, 512)
    w = a.shape[2]
    row = lambda bi, i: (bi, 0, 0)
    tile = lambda bi, i: (bi, i, 0)
    return pl.pallas_call(
        _hyb_out_kernel,
        out_shape=jax.ShapeDtypeStruct((b, n, d), F32),
        grid=(b, n // tm),
        in_specs=[pl.BlockSpec((1, tm, w), tile),
                  pl.BlockSpec((1, tm, w), tile),
                  pl.BlockSpec((1, tm, w), tile),
                  _resident(ws_pairs.shape),
                  _resident(bs_tab.shape),
                  _resident(w_bf.shape),
                  pl.BlockSpec((1, tm, d), tile),
                  pl.BlockSpec((1, 1, d), row),
                  _resident((1, d))],
        out_specs=pl.BlockSpec((1, tm, d), tile),
        scratch_shapes=[pltpu.VMEM((tm, w), BF16)],
        compiler_params=_params(2),
        name="hyb_out",
    )(a, u, vn, ws_pairs, bs_tab, w_bf, x, gate, gain)


def _ret_in_kernel(x_ref, sh_ref, sc_ref, g_ref, w_ref, cos_ref, sin_ref, o_ref, *, rope, dk):
    hb = _norm_mod(x_ref[0], g_ref[...], sh_ref[0], sc_ref[0]).astype(BF16)
    qk_w = RET_HEADS * dk
    n_out = o_ref.shape[2]
    step = 512
    for c0 in range(0, n_out, step):
        acc = jnp.dot(hb, w_ref[:, c0:c0 + step], preferred_element_type=F32)
        if c0 < 2 * qk_w:
            if c0 >= qk_w:
                acc = acc * (dk ** -0.5)
            if rope:
                parts = []
                for hd in range(step // dk):
                    t = acc[:, hd * dk:(hd + 1) * dk]
                    parts.append(t * cos_ref[...] + pltpu.roll(t, dk // 2, 1) * sin_ref[...])
                acc = jnp.concatenate(parts, axis=-1)
        elif c0 >= 2 * qk_w + (n_out - 2 * qk_w) // 2:
            acc = _silu(acc)
        o_ref[0, :, c0:c0 + step] = acc.astype(BF16)


def _ret_in(x, shift, scale, gain, w_bf, cos2, sin2, rope):
    b, n, d = x.shape
    tm = _token_tile(n, 512)
    n_out = w_bf.shape[1]
    dk = cos2.shape[1]
    row = lambda bi, i: (bi, 0, 0)
    tile = lambda bi, i: (bi, i, 0)
    return pl.pallas_call(
        functools.partial(_ret_in_kernel, rope=rope, dk=dk),
        out_shape=jax.ShapeDtypeStruct((b, n, n_out), BF16),
        grid=(b, n // tm),
        in_specs=[pl.BlockSpec((1, tm, d), tile),
                  pl.BlockSpec((1, 1, d), row),
                  pl.BlockSpec((1, 1, d), row),
                  _resident((1, d)),
                  _resident(w_bf.shape),
                  pl.BlockSpec((tm, dk), lambda bi, i: (i, 0)),
                  pl.BlockSpec((tm, dk), lambda bi, i: (i, 0))],
        out_specs=pl.BlockSpec((1, tm, n_out), tile),
        compiler_params=_params(2),
        name="ret_in",
    )(x, shift, scale, gain, w_bf, cos2, sin2)


def _ret_scan_kernel(lg_ref, q_ref, k_ref, v_ref, s0f_ref, s0b_ref, on_ref, sf_ref, sb_ref, pre_ref):
    head = pl.program_id(1)
    n, dv = v_ref.shape[1], v_ref.shape[2]
    c = RET_CHUNK
    nc = n // c
    lgf = lg_ref[0, head]
    lgb = lg_ref[1, head]
    pos = lax.broadcasted_iota(jnp.int32, (c, 1), 0).astype(F32)
    diff = (lax.broadcasted_iota(jnp.int32, (c, c), 0) - lax.broadcasted_iota(jnp.int32, (c, c), 1)).astype(F32)
    lower, upper = diff >= 0, diff <= 0
    decay = (jnp.where(lower, jnp.exp(lgf * jnp.where(lower, diff, 0.0)), 0.0)
             + jnp.where(upper, jnp.exp(lgb * jnp.where(upper, -diff, 0.0)), 0.0))
    dq_f, dk_f, dc_f = jnp.exp(lgf * (pos + 1.0)), jnp.exp(lgf * (c - 1.0 - pos)), jnp.exp(lgf * c)
    dq_b, dk_b, dc_b = jnp.exp(lgb * (c - pos)), jnp.exp(lgb * pos), jnp.exp(lgb * c)

    sf_ref[0, 0] = s0f_ref[0, 0]
    sb_ref[0, 0] = s0b_ref[0, 0]

    def advance(ci, st_ref, lanes, dk_row, dchunk):
        off = pl.multiple_of(ci * c, c)
        kd = (k_ref[0, pl.ds(off, c), :].astype(F32) * dk_row).astype(BF16)
        upd = lax.dot_general(kd, v_ref[0, pl.ds(off, c), :], TN_DIMS, preferred_element_type=F32)
        state = st_ref[0, 0]
        pre_ref[ci, :, lanes] = state.astype(BF16)
        st_ref[0, 0] = state * dchunk + upd

    def state_body(i, carry):
        advance(i, sf_ref, slice(0, dv), dk_f, dc_f)
        advance(nc - 1 - i, sb_ref, slice(dv, 2 * dv), dk_b, dc_b)
        return carry

    lax.fori_loop(0, nc, state_body, 0, unroll=min(8, nc))

    def out_body(ci, carry):
        off = pl.multiple_of(ci * c, c)
        q = q_ref[0, pl.ds(off, c), :]
        k = k_ref[0, pl.ds(off, c), :]
        v = v_ref[0, pl.ds(off, c), :]
        s = lax.dot_general(q, k, NT_DIMS, preferred_element_type=F32) * decay
        cross = jnp.dot(q, pre_ref[ci], preferred_element_type=F32)
        tot = (jnp.dot(s.astype(BF16), v, preferred_element_type=F32)
               + dq_f * cross[:, :dv] + dq_b * cross[:, dv:])
        mu = jnp.mean(tot, axis=-1, keepdims=True)
        tc = tot - mu
        var = jnp.mean(tc * tc, axis=-1, keepdims=True)
        on_ref[0, pl.ds(off, c), :] = (tc * lax.rsqrt(var + EPS)).astype(BF16)
        return carry

    lax.fori_loop(0, nc, out_body, 0, unroll=min(8, nc))


def _ret_scan(qkvg, log_decay, s0f, s0b, dk, dv):
    b, n, _ = qkvg.shape
    h = RET_HEADS
    v_blk0 = (2 * h * dk) // dv
    state = jax.ShapeDtypeStruct((b, h, dk, dv), F32)
    st_spec = pl.BlockSpec((1, 1, dk, dv), lambda bi, hi: (bi, hi, 0, 0))
    return pl.pallas_call(
        _ret_scan_kernel,
        out_shape=(jax.ShapeDtypeStruct((b, n, h * dv), BF16), state, state),
        grid=(b, h),
        in_specs=[pl.BlockSpec(memory_space=pltpu.SMEM),
                  pl.BlockSpec((1, n, dk), lambda bi, hi: (bi, 0, hi)),
                  pl.BlockSpec((1, n, dk), lambda bi, hi: (bi, 0, h + hi)),
                  pl.BlockSpec((1, n, dv), lambda bi, hi: (bi, 0, v_blk0 + hi)),
                  st_spec, st_spec],
        out_specs=(pl.BlockSpec((1, n, dv), lambda bi, hi: (bi, 0, hi)), st_spec, st_spec),
        scratch_shapes=[pltpu.VMEM((n // RET_CHUNK, dk, 2 * dv), BF16)],
        compiler_params=_params(2),
        name="ret_scan",
    )(log_decay, qkvg, qkvg, qkvg, s0f, s0b)


def _ret_out_kernel(gt_ref, on_ref, gn_ref, w_ref, x_ref, g1_ref, ng_ref, o_ref):
    z = (gt_ref[0].astype(F32) * (on_ref[0].astype(F32) * gn_ref[...])).astype(BF16)
    y = jnp.dot(z, w_ref[...], preferred_element_type=F32)
    o_ref[0] = x_ref[0] + g1_ref[0] * _rms(y, ng_ref[...])


def _ret_out(qkvg, on, gn_g, w_bf, x, gate, gain):
    b, n, d = x.shape
    tm = _token_tile(n, 512)
    vw = on.shape[2]
    g_blk = qkvg.shape[2] // vw - 1
    row = lambda bi, i: (bi, 0, 0)
    tile = lambda bi, i: (bi, i, 0)
    return pl.pallas_call(
        _ret_out_kernel,
        out_shape=jax.ShapeDtypeStruct((b, n, d), F32),
        grid=(b, n // tm),
        in_specs=[pl.BlockSpec((1, tm, vw), lambda bi, i: (bi, i, g_blk)),
                  pl.BlockSpec((1, tm, vw), tile),
                  _resident((1, vw)),
                  _resident(w_bf.shape),
                  pl.BlockSpec((1, tm, d), tile),
                  pl.BlockSpec((1, 1, d), row),
                  _resident((1, d))],
        out_specs=pl.BlockSpec((1, tm, d), tile),
        compiler_params=_params(2),
        name="ret_out",
    )(qkvg, on, gn_g, w_bf, x, gate, gain)


def _ffn_kernel(xp_ref, x_ref, xn_ref, sh_ref, sc_ref, g_ref, win_ref, cw_ref, cb_ref, wout_ref, g2_ref, ng_ref,
                o_ref, h_ref, a0_ref, a1_ref, acc_ref):
    i = pl.program_id(1)
    last = pl.num_programs(1) - 1
    tm = x_ref.shape[1]
    gain, shift, scale = g_ref[...], sh_ref[0], sc_ref[0]
    hp = jnp.where(i > 0, _norm_mod(xp_ref[0], gain, shift, scale), 0.0)
    hn = jnp.where(i < last, _norm_mod(xn_ref[0], gain, shift, scale), 0.0)
    h_ref[:HALO] = hp.astype(BF16)
    h_ref[HALO:HALO + tm] = _norm_mod(x_ref[0], gain, shift, scale).astype(BF16)
    h_ref[HALO + tm:] = hn.astype(BF16)
    fc = FFN_CHUNK
    ff = wout_ref.shape[0]
    nch = ff // fc

    def cols(ci, half):
        return pl.ds(pl.multiple_of(half * ff + ci * fc, fc), fc)

    def project(ci, a_ref):
        for half in range(2):
            a_ref[:, half * fc:(half + 1) * fc] = jnp.dot(h_ref[...], win_ref[:, cols(ci, half)],
                                                          preferred_element_type=F32)

    def conv(ci, a_ref, half):
        lanes = slice(half * fc, (half + 1) * fc)
        cw = cw_ref[:, cols(ci, half)]
        return (a_ref[pl.ds(HALO - 1, tm), lanes] * cw[0:1] + a_ref[pl.ds(HALO, tm), lanes] * cw[1:2]
                + a_ref[pl.ds(HALO + 1, tm), lanes] * cw[2:3] + cb_ref[:, cols(ci, half)])

    def mix(ci, a_ref):
        act = (_silu(conv(ci, a_ref, 0)) * conv(ci, a_ref, 1)).astype(BF16)
        acc_ref[...] += jnp.dot(act, wout_ref[pl.ds(pl.multiple_of(ci * fc, fc), fc), :],
                                preferred_element_type=F32)

    acc_ref[...] = jnp.zeros_like(acc_ref)
    project(0, a0_ref)

    def pair_body(j, carry):
        project(2 * j + 1, a1_ref)
        mix(2 * j, a0_ref)
        project(2 * j + 2, a0_ref)
        mix(2 * j + 1, a1_ref)
        return carry

    assert nch % 2 == 1
    lax.fori_loop(0, nch // 2, pair_body, 0)
    mix(nch - 1, a0_ref)
    o_ref[0] = x_ref[0] + g2_ref[0] * _rms(acc_ref[...], ng_ref[...])


def _conv_ffn(x, shift, scale, gain, win_bf, conv_w, conv_b, wout_bf, gate, gain_out):
    b, n, d = x.shape
    tm = _token_tile(n, 1024)
    assert tm % HALO == 0
    ff = wout_bf.shape[0]
    assert ff % FFN_CHUNK == 0 and (ff // FFN_CHUNK) % 2 == 1
    hb = tm // HALO
    n_hb = n // HALO
    row = lambda bi, i: (bi, 0, 0)
    tile = lambda bi, i: (bi, i, 0)
    return pl.pallas_call(
        _ffn_kernel,
        out_shape=jax.ShapeDtypeStruct((b, n, d), F32),
        grid=(b, n // tm),
        in_specs=[pl.BlockSpec((1, HALO, d), lambda bi, i: (bi, jnp.maximum(i * hb - 1, 0), 0)),
                  pl.BlockSpec((1, tm, d), tile),
                  pl.BlockSpec((1, HALO, d), lambda bi, i: (bi, jnp.minimum((i + 1) * hb, n_hb - 1), 0)),
                  pl.BlockSpec((1, 1, d), row),
                  pl.BlockSpec((1, 1, d), row),
                  _resident((1, d)),
                  _resident(win_bf.shape),
                  _resident(conv_w.shape),
                  _resident(conv_b.shape),
                  _resident(wout_bf.shape),
                  pl.BlockSpec((1, 1, d), row),
                  _resident((1, d))],
        out_specs=pl.BlockSpec((1, tm, d), tile),
        scratch_shapes=[pltpu.VMEM((tm + 2 * HALO, d), BF16),
                        pltpu.VMEM((tm + 2 * HALO, 2 * FFN_CHUNK), F32),
                        pltpu.VMEM((tm + 2 * HALO, 2 * FFN_CHUNK), F32),
                        pltpu.VMEM((tm, d), F32)],
        compiler_params=_params(2),
        name="conv_ffn",
    )(x, x, x, shift, scale, gain, win_bf, conv_w, conv_b, wout_bf, gate, gain_out)


def _rope_tables(t, dk):
    tok = jnp.arange(t)
    row = (tok // GRID_W).astype(F32)
    col = (tok % GRID_W).astype(F32)
    n_freq = dk // 4
    inv = ROPE_BASE ** (-jnp.arange(n_freq, dtype=F32) / n_freq)
    ang = jnp.concatenate([row[:, None] * inv, col[:, None] * inv], axis=-1)
    cos, sin = jnp.cos(ang), jnp.sin(ang)
    return jnp.concatenate([cos, cos], axis=-1), jnp.concatenate([-sin, sin], axis=-1)


def kernel(x, c, ctx, c_ctx, ada_w, ada_b, norm_g, hyb_w_in, na_rpb, sgu_w, sgu_b, hyb_w_out, ret_w_in, ret_log_decay,
           ret_gn_g, ret_w_out, ffn_w_in, ffn_conv_w, ffn_conv_b, ffn_w_out):
    b, t, d = x.shape
    depth = ada_w.shape[0]
    dk = d // RET_HEADS
    dv = 2 * dk

    cond = jnp.zeros((8, d), F32).at[:b].set(c).at[b].set(c_ctx)
    mods = _ada_rows(cond, ada_w, ada_b)
    cos2, sin2 = _rope_tables(t, dk)
    xc = ctx

    for i in range(depth):
        ctx_out = i < depth - 1
        j = i // 2
        lat = [mods[i, :b, m * d:(m + 1) * d][:, None, :] for m in range(6)]
        cxm = [jnp.broadcast_to(mods[i, b, m * d:(m + 1) * d][None, None, :], (b, 1, d)) for m in range(6)]
        gains = [norm_g[i, m][None, :] for m in range(4)]

        if i % 2 == 0:
            w_in = hyb_w_in[j].astype(BF16)
            w_out = hyb_w_out[j].astype(BF16)
            ws_pairs = sgu_w[j].astype(BF16).reshape(SGU_GROUPS // 2, 2 * SGU_CHUNK, SGU_CHUNK)
            bs_tab = jnp.repeat(sgu_b[j].T, (d - NA_WIDTH) // SGU_GROUPS, axis=1)
            bias = _na_bias_tables(na_rpb[j])
            qkv_c, u_c, vn_c = _hyb_in(xc, cxm[0], cxm[1], gains[0], w_in)
            qkv_l, u_l, vn_l = _hyb_in(x, lat[0], lat[1], gains[0], w_in)
            a_l = _na_attention(qkv_l, qkv_c, bias)
            x = _hyb_out(a_l, u_l, vn_l, ws_pairs, bs_tab, w_out, x, lat[2], gains[1])
            if ctx_out:
                a_c = _ctx_attention(qkv_c)
                xc = _hyb_out(a_c, u_c, vn_c, ws_pairs, bs_tab, w_out, xc, cxm[2], gains[1])
        else:
            w_in = ret_w_in[j].astype(BF16)
            w_out = ret_w_out[j].astype(BF16)
            lg = ret_log_decay[j].astype(F32)
            gn = ret_gn_g[j][None, :]
            zero_state = jnp.zeros((b, RET_HEADS, dk, dv), F32)
            p_c = _ret_in(xc, cxm[0], cxm[1], gains[0], w_in, cos2, sin2, rope=False)
            p_l = _ret_in(x, lat[0], lat[1], gains[0], w_in, cos2, sin2, rope=True)
            on_c, sf, sb = _ret_scan(p_c, lg, zero_state, zero_state, dk, dv)
            on_l, _, _ = _ret_scan(p_l, lg, sf, sb, dk, dv)
            x = _ret_out(p_l, on_l, gn, w_out, x, lat[2], gains[1])
            if ctx_out:
                xc = _ret_out(p_c, on_c, gn, w_out, xc, cxm[2], gains[1])

        ffn_w = (ffn_w_in[i].astype(BF16), ffn_conv_w[i], ffn_conv_b[i][None, :], ffn_w_out[i].astype(BF16))
        x = _conv_ffn(x, lat[3], lat[4], gains[2], *ffn_w, lat[5], gains[3])
        if ctx_out:
            xc = _conv_ffn(xc, cxm[3], cxm[4], gains[2], *ffn_w, cxm[5], gains[3])
    return x
```

```python
import functools

import jax
import jax.numpy as jnp
from jax import lax
from jax.experimental import pallas as pl
from jax.experimental.pallas import tpu as pltpu

F32 = jnp.float32
BF16 = jnp.bfloat16

GRID_W = 64
EPS = 1e-6
NEG_INF = -1e30
LOG2E = 1.4426950408889634

NA_HEADS = 8
NA_HEAD_DIM = 64
NA_WIDTH = NA_HEADS * NA_HEAD_DIM
NA_KH = 8
NA_KW = 16
NA_GROUP = 4
SGU_GROUPS = 8
SGU_CHUNK = 128

RET_HEADS = 8
RET_CHUNK = 256
ROPE_BASE = 10000.0

CONV_W = 3
FFN_CHUNK = 256
HALO = 16

V7X_VMEM_LIMIT_BYTES = 56 * 1024 * 1024

NT_DIMS = (((1,), (1,)), ((), ()))
TN_DIMS = (((0,), (0,)), ((), ()))


def _params(n_axes):
    return pltpu.CompilerParams(dimension_semantics=("arbitrary",) * n_axes,
                                vmem_limit_bytes=V7X_VMEM_LIMIT_BYTES)


def _resident(shape):
    zeros = (0,) * len(shape)
    return pl.BlockSpec(shape, lambda *_: zeros, pipeline_mode=pl.Buffered(1))


def _layer_resident(stacked_shape, layer):
    tail = (0,) * (len(stacked_shape) - 1)
    return pl.BlockSpec((None,) + tuple(stacked_shape[1:]), lambda *_: (layer,) + tail, pipeline_mode=pl.Buffered(1))


def _token_tile(n, want):
    tm = min(n, want)
    assert n % tm == 0
    return tm


def _rms(v, gain):
    return v * lax.rsqrt(jnp.mean(v * v, axis=-1, keepdims=True) + EPS) * gain


def _silu(v):
    return v * (1.0 / (1.0 + jnp.exp(-v)))


def _norm_mod(x, gain, shift, scale):
    return _rms(x, gain) * (1.0 + scale) + shift


def _ada_kernel(c_ref, w_ref, b_ref, o_ref):
    s = _silu(c_ref[...])
    o_ref[0] = jnp.dot(s, w_ref[0], preferred_element_type=F32) + b_ref[0]


def _ada_rows(cond, ada_w, ada_b):
    depth, d, n = ada_w.shape
    tn = 2048
    return pl.pallas_call(
        _ada_kernel,
        out_shape=jax.ShapeDtypeStruct((depth, 8, n), F32),
        grid=(depth, n // tn),
        in_specs=[pl.BlockSpec((8, d), lambda i, j: (0, 0)),
                  pl.BlockSpec((1, d, tn), lambda i, j: (i, 0, j)),
                  pl.BlockSpec((1, 1, tn), lambda i, j: (i, 0, j))],
        out_specs=pl.BlockSpec((1, 8, tn), lambda i, j: (i, 0, j)),
        compiler_params=_params(2),
        name="ada_rows",
    )(cond, ada_w, ada_b.reshape(depth, 1, n))


def _hyb_in_kernel(x_ref, sh_ref, sc_ref, g_ref, w_ref, qkv_ref, u_ref, vn_ref):
    hb = _norm_mod(x_ref[0], g_ref[...], sh_ref[0], sc_ref[0]).astype(BF16)
    w = NA_WIDTH
    for part in range(3):
        acc = jnp.dot(hb, w_ref[:, part * w:(part + 1) * w], preferred_element_type=F32)
        if part == 0:
            acc = acc * (NA_HEAD_DIM ** -0.5 * LOG2E)
        qkv_ref[0, :, part * w:(part + 1) * w] = acc.astype(BF16)
    u = jax.nn.gelu(jnp.dot(hb, w_ref[:, 3 * w:4 * w], preferred_element_type=F32))
    u_ref[0] = u.astype(BF16)
    g = jax.nn.gelu(jnp.dot(hb, w_ref[:, 4 * w:5 * w], preferred_element_type=F32))
    mu = jnp.mean(g, axis=-1, keepdims=True)
    gc = g - mu
    var = jnp.mean(gc * gc, axis=-1, keepdims=True)
    vn_ref[0] = (gc * lax.rsqrt(var + EPS)).astype(BF16)


def _hyb_in(x, shift, scale, gain, w_bf, layer):
    b, n, d = x.shape
    tm = _token_tile(n, 512)
    w = NA_WIDTH
    row = lambda bi, i: (bi, 0, 0)
    tile = lambda bi, i: (bi, i, 0)
    return pl.pallas_call(
        _hyb_in_kernel,
        out_shape=(jax.ShapeDtypeStruct((b, n, 3 * w), BF16),
                   jax.ShapeDtypeStruct((b, n, w), BF16),
                   jax.ShapeDtypeStruct((b, n, w), BF16)),
        grid=(b, n // tm),
        in_specs=[pl.BlockSpec((1, tm, d), tile),
                  pl.BlockSpec((1, 1, d), row),
                  pl.BlockSpec((1, 1, d), row),
                  _resident((1, d)),
                  _layer_resident(w_bf.shape, layer)],
        out_specs=(pl.BlockSpec((1, tm, 3 * w), tile),
                   pl.BlockSpec((1, tm, w), tile),
                   pl.BlockSpec((1, tm, w), tile)),
        compiler_params=_params(2),
        name="hyb_in",
    )(x, shift, scale, gain, w_bf)


def _split_heads(qg):
    lane = lax.broadcasted_iota(jnp.int32, qg.shape, 1)
    zero = jnp.zeros_like(qg)
    return jnp.concatenate([jnp.where(lane // NA_HEAD_DIM == hd, qg, zero) for hd in range(NA_GROUP)], axis=0)


def _merge_heads(og):
    m = og.shape[0] // NA_GROUP
    lane = lax.broadcasted_iota(jnp.int32, (m, og.shape[1]), 1)
    out = og[:m]
    for hd in range(1, NA_GROUP):
        out = jnp.where(lane // NA_HEAD_DIM == hd, og[hd * m:(hd + 1) * m], out)
    return out


def _fold_lanes(v, op, width=128):
    out = v[:, :width]
    for c0 in range(width, v.shape[1], width):
        out = op(out, v[:, c0:c0 + width])
    return out


def _na_kernel(q_ref, k_ref, v_ref, kc_ref, vc_ref, bias_ref, o_ref, snb0_ref, scx0_ref, snb1_ref, scx1_ref, *, rows):
    kh = NA_KH
    span = kh * GRID_W
    bufs = ((snb0_ref, scx0_ref), (snb1_ref, scx1_ref))

    def window(r):
        r0 = jnp.clip(r - kh // 2, 0, rows - kh)
        return r0, pl.multiple_of(r * GRID_W, GRID_W), pl.multiple_of(r0 * GRID_W, GRID_W)

    def scores(r, buf):
        snb_ref, scx_ref = buf
        r0, q_off, k_off = window(r)
        qs = _split_heads(q_ref[0, pl.ds(q_off, GRID_W), :])
        s = lax.dot_general(qs, k_ref[0, pl.ds(k_off, span), :], NT_DIMS, preferred_element_type=F32)
        dr0 = kh - 1 - (r - r0)
        base = jnp.where(dr0 % 2 == 0, dr0 // 2, kh + (dr0 - 1) // 2)
        for jj in range(kh // 2):
            lanes = slice(jj * 2 * GRID_W, (jj + 1) * 2 * GRID_W)
            snb_ref[:, lanes] = s[:, lanes] + bias_ref[0, base + jj]
        scx_ref[...] = lax.dot_general(qs, kc_ref[0], NT_DIMS, preferred_element_type=F32)

    def attend(r, buf):
        snb_ref, scx_ref = buf
        _, q_off, k_off = window(r)
        s_nb = snb_ref[...]
        s_cx = scx_ref[...]
        m = jnp.max(jnp.maximum(_fold_lanes(s_nb, jnp.maximum), _fold_lanes(s_cx, jnp.maximum)),
                    axis=-1, keepdims=True)
        p_nb = jnp.exp2(s_nb - m)
        p_cx = jnp.exp2(s_cx - m)
        denom = jnp.sum(_fold_lanes(p_nb, jnp.add) + _fold_lanes(p_cx, jnp.add), axis=-1, keepdims=True)
        o2 = (jnp.dot(p_nb.astype(BF16), v_ref[0, pl.ds(k_off, span), :], preferred_element_type=F32)
              + jnp.dot(p_cx.astype(BF16), vc_ref[0], preferred_element_type=F32)) / denom
        o_ref[0, pl.ds(q_off, GRID_W), :] = _merge_heads(o2).astype(BF16)

    scores(0, bufs[0])

    def pair_body(j, carry):
        r = 2 * j
        scores(r + 1, bufs[1])
        attend(r, bufs[0])
        scores(jnp.minimum(r + 2, rows - 1), bufs[0])
        attend(r + 1, bufs[1])
        return carry

    assert rows % 2 == 0
    lax.fori_loop(0, rows // 2, pair_body, 0, unroll=2)


def _na_bias_tables(rpb):
    h = rpb.shape[0]
    w, kw = GRID_W, NA_KW
    cidx = jnp.arange(w)
    c0 = jnp.clip(cidx - kw // 2, 0, w - kw)
    in_win = (cidx[None, :] >= c0[:, None]) & (cidx[None, :] < c0[:, None] + kw)
    edge = w - kw
    ext = jnp.concatenate([jnp.broadcast_to(rpb[..., :1], rpb.shape[:-1] + (edge,)), rpb.astype(F32),
                           jnp.broadcast_to(rpb[..., -1:], rpb.shape[:-1] + (edge,))], axis=-1)
    ext = jnp.concatenate([ext, jnp.zeros_like(ext[..., :1])], axis=-1)
    flow = jnp.tile(ext, (1, 1, w))[..., :w * (2 * w - 1)].reshape(ext.shape[:-1] + (w, 2 * w - 1))
    toep = flow[..., w - 1:]
    toep = jnp.where(in_win, toep, NEG_INF) * LOG2E
    g = NA_GROUP
    n_dr = toep.shape[1]
    tg = toep.reshape(h // g, g, n_dr, w, w).transpose(0, 2, 1, 3, 4).reshape(h // g, n_dr, g * w, w)
    tg = jnp.concatenate([tg, jnp.zeros_like(tg[:, :1])], axis=1)
    even = jnp.concatenate([tg[:, 0::2], tg[:, 1::2]], axis=-1)
    odd = jnp.concatenate([tg[:, 1:-1:2], tg[:, 2::2]], axis=-1)
    return jnp.concatenate([even, odd], axis=1)


def _na_attention(qkv, qkv_ctx, bias):
    b, t, _ = qkv.shape
    l = qkv_ctx.shape[1]
    rows = t // GRID_W
    assert rows >= NA_KH
    groups = NA_HEADS // NA_GROUP
    lanes = NA_GROUP * NA_HEAD_DIM
    return pl.pallas_call(
        functools.partial(_na_kernel, rows=rows),
        out_shape=jax.ShapeDtypeStruct((b, t, NA_WIDTH), BF16),
        grid=(b, groups),
        in_specs=[pl.BlockSpec((1, t, lanes), lambda bi, hg: (bi, 0, hg)),
                  pl.BlockSpec((1, t, lanes), lambda bi, hg: (bi, 0, groups + hg)),
                  pl.BlockSpec((1, t, lanes), lambda bi, hg: (bi, 0, 2 * groups + hg)),
                  pl.BlockSpec((1, l, lanes), lambda bi, hg: (bi, 0, groups + hg)),
                  pl.BlockSpec((1, l, lanes), lambda bi, hg: (bi, 0, 2 * groups + hg)),
                  pl.BlockSpec((1,) + bias.shape[1:], lambda bi, hg: (hg, 0, 0, 0))],
        out_specs=pl.BlockSpec((1, t, lanes), lambda bi, hg: (bi, 0, hg)),
        scratch_shapes=[pltpu.VMEM((NA_GROUP * GRID_W, NA_KH * GRID_W), F32),
                        pltpu.VMEM((NA_GROUP * GRID_W, l), F32)] * 2,
        compiler_params=_params(2),
        name="na_attention",
    )(qkv, qkv, qkv, qkv_ctx, qkv_ctx, bias)


def _ctx_attn_kernel(q_ref, k_ref, v_ref, o_ref):
    qs = _split_heads(q_ref[0])
    s = lax.dot_general(qs, k_ref[0], NT_DIMS, preferred_element_type=F32)
    p = jnp.exp2(s - jnp.max(s, axis=-1, keepdims=True))
    denom = jnp.sum(p, axis=-1, keepdims=True)
    o2 = jnp.dot(p.astype(BF16), v_ref[0], preferred_element_type=F32) / denom
    o_ref[0] = _merge_heads(o2).astype(BF16)


def _ctx_attention(qkv_ctx):
    b, l, _ = qkv_ctx.shape
    groups = NA_HEADS // NA_GROUP
    lanes = NA_GROUP * NA_HEAD_DIM
    return pl.pallas_call(
        _ctx_attn_kernel,
        out_shape=jax.ShapeDtypeStruct((b, l, NA_WIDTH), BF16),
        grid=(b, groups),
        in_specs=[pl.BlockSpec((1, l, lanes), lambda bi, hg: (bi, 0, hg)),
                  pl.BlockSpec((1, l, lanes), lambda bi, hg: (bi, 0, groups + hg)),
                  pl.BlockSpec((1, l, lanes), lambda bi, hg: (bi, 0, 2 * groups + hg))],
        out_specs=pl.BlockSpec((1, l, lanes), lambda bi, hg: (bi, 0, hg)),
        compiler_params=_params(2),
        name="ctx_attention",
    )(qkv_ctx, qkv_ctx, qkv_ctx)


def _hyb_out_kernel(a_ref, u_ref, vn_ref, ws_ref, bs_ref, w_ref, x_ref, g1_ref, ng_ref, o_ref, s_ref):
    tm = a_ref.shape[1]
    c = SGU_CHUNK
    gd = vn_ref.shape[2] // SGU_GROUPS
    for ci in range(tm // c):
        rows = slice(ci * c, (ci + 1) * c)
        for gp in range(SGU_GROUPS // 2):
            lanes = slice(gp * 2 * gd, (gp + 1) * 2 * gd)
            mixed2 = jnp.dot(ws_ref[gp], vn_ref[0, rows, lanes], preferred_element_type=F32)
            lane = lax.broadcasted_iota(jnp.int32, (c, 2 * gd), 1)
            mixed = jnp.where(lane < gd, mixed2[:c], mixed2[c:]) + bs_ref[:, lanes]
            s_ref[rows, lanes] = (u_ref[0, rows, lanes].astype(F32) * mixed).astype(BF16)
    wa = a_ref.shape[2]
    y = (jnp.dot(a_ref[0], w_ref[:wa, :], preferred_element_type=F32)
         + jnp.dot(s_ref[...], w_ref[wa:, :], preferred_element_type=F32))
    o_ref[0] = x_ref[0] + g1_ref[0] * _rms(y, ng_ref[...])


def _hyb_out(a, u, vn, ws_pairs, bs_tab, w_bf, layer, x, gate, gain):
    b, n, d = x.shape
    tm = _token_tile(n, 512)
    w = a.shape[2]
    row = lambda bi, i: (bi, 0, 0)
    tile = lambda bi, i: (bi, i, 0)
    return pl.pallas_call(
        _hyb_out_kernel,
        out_shape=jax.ShapeDtypeStruct((b, n, d), F32),
        grid=(b, n // tm),
        in_specs=[pl.BlockSpec((1, tm, w), tile),
                  pl.BlockSpec((1, tm, w), tile),
                  pl.BlockSpec((1, tm, w), tile),
                  _resident(ws_pairs.shape),
                  _resident(bs_tab.shape),
                  _layer_resident(w_bf.shape, layer),
                  pl.BlockSpec((1, tm, d), tile),
                  pl.BlockSpec((1, 1, d), row),
                  _resident((1, d))],
        out_specs=pl.BlockSpec((1, tm, d), tile),
        scratch_shapes=[pltpu.VMEM((tm, w), BF16)],
        compiler_params=_params(2),
        name="hyb_out",
    )(a, u, vn, ws_pairs, bs_tab, w_bf, x, gate, gain)


def _ret_in_kernel(x_ref, sh_ref, sc_ref, g_ref, w_ref, cos_ref, sin_ref, o_ref, *, rope, dk):
    hb = _norm_mod(x_ref[0], g_ref[...], sh_ref[0], sc_ref[0]).astype(BF16)
    qk_w = RET_HEADS * dk
    n_out = o_ref.shape[2]
    step = 512
    for c0 in range(0, n_out, step):
        acc = jnp.dot(hb, w_ref[:, c0:c0 + step], preferred_element_type=F32)
        if c0 < 2 * qk_w:
            if c0 >= qk_w:
                acc = acc * (dk ** -0.5)
            if rope:
                parts = []
                for hd in range(step // dk):
                    t = acc[:, hd * dk:(hd + 1) * dk]
                    parts.append(t * cos_ref[...] + pltpu.roll(t, dk // 2, 1) * sin_ref[...])
                acc = jnp.concatenate(parts, axis=-1)
        elif c0 >= 2 * qk_w + (n_out - 2 * qk_w) // 2:
            acc = _silu(acc)
        o_ref[0, :, c0:c0 + step] = acc.astype(BF16)


def _ret_in(x, shift, scale, gain, w_bf, layer, cos2, sin2, rope):
    b, n, d = x.shape
    tm = _token_tile(n, 512)
    n_out = w_bf.shape[2]
    dk = cos2.shape[1]
    row = lambda bi, i: (bi, 0, 0)
    tile = lambda bi, i: (bi, i, 0)
    return pl.pallas_call(
        functools.partial(_ret_in_kernel, rope=rope, dk=dk),
        out_shape=jax.ShapeDtypeStruct((b, n, n_out), BF16),
        grid=(b, n // tm),
        in_specs=[pl.BlockSpec((1, tm, d), tile),
                  pl.BlockSpec((1, 1, d), row),
                  pl.BlockSpec((1, 1, d), row),
                  _resident((1, d)),
                  _layer_resident(w_bf.shape, layer),
                  pl.BlockSpec((tm, dk), lambda bi, i: (i, 0)),
                  pl.BlockSpec((tm, dk), lambda bi, i: (i, 0))],
        out_specs=pl.BlockSpec((1, tm, n_out), tile),
        compiler_params=_params(2),
        name="ret_in",
    )(x, shift, scale, gain, w_bf, cos2, sin2)


def _ret_scan_kernel(lg_ref, q_ref, k_ref, v_ref, s0f_ref, s0b_ref, on_ref, sf_ref, sb_ref, pre_ref):
    head = pl.program_id(1)
    n, dv = v_ref.shape[1], v_ref.shape[2]
    dk = q_ref.shape[2]
    c = RET_CHUNK
    nc = n // c
    lgf = lg_ref[0, head]
    lgb = lg_ref[1, head]
    pos = lax.broadcasted_iota(jnp.int32, (c, 1), 0).astype(F32)
    diff = (lax.broadcasted_iota(jnp.int32, (c, c), 0) - lax.broadcasted_iota(jnp.int32, (c, c), 1)).astype(F32)
    lower, upper = diff >= 0, diff <= 0
    decay = (jnp.where(lower, jnp.exp(lgf * jnp.where(lower, diff, 0.0)), 0.0)
             + jnp.where(upper, jnp.exp(lgb * jnp.where(upper, -diff, 0.0)), 0.0))
    dq_f, dk_f, dc_f = jnp.exp(lgf * (pos + 1.0)), jnp.exp(lgf * (c - 1.0 - pos)), jnp.exp(lgf * c)
    dq_b, dk_b, dc_b = jnp.exp(lgb * (c - pos)), jnp.exp(lgb * pos), jnp.exp(lgb * c)

    sf_ref[0, 0] = s0f_ref[0, 0]
    sb_ref[0, 0] = s0b_ref[0, 0]

    def advance(ci, st_ref, rows, dk_row, dchunk):
        off = pl.multiple_of(ci * c, c)
        kd = (k_ref[0, pl.ds(off, c), :].astype(F32) * dk_row).astype(BF16)
        upd = lax.dot_general(kd, v_ref[0, pl.ds(off, c), :], TN_DIMS, preferred_element_type=F32)
        state = st_ref[0, 0]
        pre_ref[ci, rows, :] = state.astype(BF16)
        st_ref[0, 0] = state * dchunk + upd

    def state_body(i, carry):
        advance(i, sf_ref, slice(0, dk), dk_f, dc_f)
        advance(nc - 1 - i, sb_ref, slice(dk, 2 * dk), dk_b, dc_b)
        return carry

    lax.fori_loop(0, nc, state_body, 0, unroll=min(8, nc))

    def out_body(ci, carry):
        off = pl.multiple_of(ci * c, c)
        q = q_ref[0, pl.ds(off, c), :]
        k = k_ref[0, pl.ds(off, c), :]
        v = v_ref[0, pl.ds(off, c), :]
        s = lax.dot_general(q, k, NT_DIMS, preferred_element_type=F32) * decay
        qf = q.astype(F32)
        qq = jnp.concatenate([(qf * dq_f).astype(BF16), (qf * dq_b).astype(BF16)], axis=-1)
        tot = (jnp.dot(s.astype(BF16), v, preferred_element_type=F32)
               + jnp.dot(qq, pre_ref[ci], preferred_element_type=F32))
        mu = jnp.mean(tot, axis=-1, keepdims=True)
        tc = tot - mu
        var = jnp.mean(tc * tc, axis=-1, keepdims=True)
        on_ref[0, pl.ds(off, c), :] = (tc * lax.rsqrt(var + EPS)).astype(BF16)
        return carry

    lax.fori_loop(0, nc, out_body, 0, unroll=min(8, nc))


def _ret_scan(qkvg, log_decay, s0f, s0b, dk, dv):
    b, n, _ = qkvg.shape
    h = RET_HEADS
    v_blk0 = (2 * h * dk) // dv
    state = jax.ShapeDtypeStruct((b, h, dk, dv), F32)
    st_spec = pl.BlockSpec((1, 1, dk, dv), lambda bi, hi: (bi, hi, 0, 0))
    return pl.pallas_call(
        _ret_scan_kernel,
        out_shape=(jax.ShapeDtypeStruct((b, n, h * dv), BF16), state, state),
        grid=(b, h),
        in_specs=[pl.BlockSpec(memory_space=pltpu.SMEM),
                  pl.BlockSpec((1, n, dk), lambda bi, hi: (bi, 0, hi)),
                  pl.BlockSpec((1, n, dk), lambda bi, hi: (bi, 0, h + hi)),
                  pl.BlockSpec((1, n, dv), lambda bi, hi: (bi, 0, v_blk0 + hi)),
                  st_spec, st_spec],
        out_specs=(pl.BlockSpec((1, n, dv), lambda bi, hi: (bi, 0, hi)), st_spec, st_spec),
        scratch_shapes=[pltpu.VMEM((n // RET_CHUNK, 2 * dk, dv), BF16)],
        compiler_params=_params(2),
        name="ret_scan",
    )(log_decay, qkvg, qkvg, qkvg, s0f, s0b)


def _ret_out_kernel(gt_ref, on_ref, gn_ref, w_ref, x_ref, g1_ref, ng_ref, o_ref):
    z = (gt_ref[0].astype(F32) * (on_ref[0].astype(F32) * gn_ref[...])).astype(BF16)
    y = jnp.dot(z, w_ref[...], preferred_element_type=F32)
    o_ref[0] = x_ref[0] + g1_ref[0] * _rms(y, ng_ref[...])


def _ret_out(qkvg, on, gn_g, w_bf, layer, x, gate, gain):
    b, n, d = x.shape
    tm = _token_tile(n, 512)
    vw = on.shape[2]
    g_blk = qkvg.shape[2] // vw - 1
    row = lambda bi, i: (bi, 0, 0)
    tile = lambda bi, i: (bi, i, 0)
    return pl.pallas_call(
        _ret_out_kernel,
        out_shape=jax.ShapeDtypeStruct((b, n, d), F32),
        grid=(b, n // tm),
        in_specs=[pl.BlockSpec((1, tm, vw), lambda bi, i: (bi, i, g_blk)),
                  pl.BlockSpec((1, tm, vw), tile),
                  _resident((1, vw)),
                  _layer_resident(w_bf.shape, layer),
                  pl.BlockSpec((1, tm, d), tile),
                  pl.BlockSpec((1, 1, d), row),
                  _resident((1, d))],
        out_specs=pl.BlockSpec((1, tm, d), tile),
        compiler_params=_params(2),
        name="ret_out",
    )(qkvg, on, gn_g, w_bf, x, gate, gain)


def _ffn_kernel(xp_ref, x_ref, xn_ref, sh_ref, sc_ref, g_ref, win_ref, cw_ref, cb_ref, wout_ref, g2_ref, ng_ref,
                o_ref, h_ref, a0_ref, a1_ref, acc_ref):
    i = pl.program_id(1)
    last = pl.num_programs(1) - 1
    tm = x_ref.shape[1]
    gain, shift, scale = g_ref[...], sh_ref[0], sc_ref[0]
    hp = jnp.where(i > 0, _norm_mod(xp_ref[0], gain, shift, scale), 0.0)
    hn = jnp.where(i < last, _norm_mod(xn_ref[0], gain, shift, scale), 0.0)
    h_ref[:HALO] = hp.astype(BF16)
    h_ref[HALO:HALO + tm] = _norm_mod(x_ref[0], gain, shift, scale).astype(BF16)
    h_ref[HALO + tm:] = hn.astype(BF16)
    fc = FFN_CHUNK
    ff = wout_ref.shape[0]
    nch = ff // fc

    def cols(ci, half):
        return pl.ds(pl.multiple_of(half * ff + ci * fc, fc), fc)

    def project(ci, a_ref):
        for half in range(2):
            a_ref[:, half * fc:(half + 1) * fc] = jnp.dot(h_ref[...], win_ref[:, cols(ci, half)],
                                                          preferred_element_type=F32)

    def conv(ci, a_ref, half):
        lanes = slice(half * fc, (half + 1) * fc)
        cw = cw_ref[:, cols(ci, half)]
        return (a_ref[pl.ds(HALO - 1, tm), lanes] * cw[0:1] + a_ref[pl.ds(HALO, tm), lanes] * cw[1:2]
                + a_ref[pl.ds(HALO + 1, tm), lanes] * cw[2:3] + cb_ref[:, cols(ci, half)])

    def mix(ci, a_ref):
        act = (_silu(conv(ci, a_ref, 0)) * conv(ci, a_ref, 1)).astype(BF16)
        acc_ref[...] += jnp.dot(act, wout_ref[pl.ds(pl.multiple_of(ci * fc, fc), fc), :],
                                preferred_element_type=F32)

    acc_ref[...] = jnp.zeros_like(acc_ref)
    project(0, a0_ref)

    def pair_body(j, carry):
        project(2 * j + 1, a1_ref)
        mix(2 * j, a0_ref)
        project(2 * j + 2, a0_ref)
        mix(2 * j + 1, a1_ref)
        return carry

    assert nch % 2 == 1
    lax.fori_loop(0, nch // 2, pair_body, 0)
    mix(nch - 1, a0_ref)
    o_ref[0] = x_ref[0] + g2_ref[0] * _rms(acc_ref[...], ng_ref[...])


def _conv_ffn(x, shift, scale, gain, win_bf, conv_w, conv_b, wout_bf, layer, gate, gain_out):
    b, n, d = x.shape
    tm = _token_tile(n, 1024)
    assert tm % HALO == 0
    ff = wout_bf.shape[1]
    assert ff % FFN_CHUNK == 0 and (ff // FFN_CHUNK) % 2 == 1
    hb = tm // HALO
    n_hb = n // HALO
    row = lambda bi, i: (bi, 0, 0)
    tile = lambda bi, i: (bi, i, 0)
    return pl.pallas_call(
        _ffn_kernel,
        out_shape=jax.ShapeDtypeStruct((b, n, d), F32),
        grid=(b, n // tm),
        in_specs=[pl.BlockSpec((1, HALO, d), lambda bi, i: (bi, jnp.maximum(i * hb - 1, 0), 0)),
                  pl.BlockSpec((1, tm, d), tile),
                  pl.BlockSpec((1, HALO, d), lambda bi, i: (bi, jnp.minimum((i + 1) * hb, n_hb - 1), 0)),
                  pl.BlockSpec((1, 1, d), row),
                  pl.BlockSpec((1, 1, d), row),
                  _resident((1, d)),
                  _layer_resident(win_bf.shape, layer),
                  _layer_resident(conv_w.shape, layer),
                  _layer_resident(conv_b.shape, layer),
                  _layer_resident(wout_bf.shape, layer),
                  pl.BlockSpec((1, 1, d), row),
                  _resident((1, d))],
        out_specs=pl.BlockSpec((1, tm, d), tile),
        scratch_shapes=[pltpu.VMEM((tm + 2 * HALO, d), BF16),
                        pltpu.VMEM((tm + 2 * HALO, 2 * FFN_CHUNK), F32),
                        pltpu.VMEM((tm + 2 * HALO, 2 * FFN_CHUNK), F32),
                        pltpu.VMEM((tm, d), F32)],
        compiler_params=_params(2),
        name="conv_ffn",
    )(x, x, x, shift, scale, gain, win_bf, conv_w, conv_b, wout_bf, gate, gain_out)


def _rope_tables(t, dk):
    tok = jnp.arange(t)
    row = (tok // GRID_W).astype(F32)
    col = (tok % GRID_W).astype(F32)
    n_freq = dk // 4
    inv = ROPE_BASE ** (-jnp.arange(n_freq, dtype=F32) / n_freq)
    ang = jnp.concatenate([row[:, None] * inv, col[:, None] * inv], axis=-1)
    cos, sin = jnp.cos(ang), jnp.sin(ang)
    return jnp.concatenate([cos, cos], axis=-1), jnp.concatenate([-sin, sin], axis=-1)


def kernel(x, c, ctx, c_ctx, ada_w, ada_b, norm_g, hyb_w_in, na_rpb, sgu_w, sgu_b, hyb_w_out, ret_w_in, ret_log_decay,
           ret_gn_g, ret_w_out, ffn_w_in, ffn_conv_w, ffn_conv_b, ffn_w_out):
    b, t, d = x.shape
    depth = ada_w.shape[0]
    dk = d // RET_HEADS
    dv = 2 * dk

    cond = jnp.zeros((8, d), F32).at[:b].set(c).at[b].set(c_ctx)
    mods = _ada_rows(cond, ada_w, ada_b)
    cos2, sin2 = _rope_tables(t, dk)
    xc = ctx
    hyb_in_bf, hyb_out_bf = hyb_w_in.astype(BF16), hyb_w_out.astype(BF16)
    ret_in_bf, ret_out_bf = ret_w_in.astype(BF16), ret_w_out.astype(BF16)
    ffn_w = (ffn_w_in.astype(BF16), ffn_conv_w, ffn_conv_b[:, None, :], ffn_w_out.astype(BF16))

    for i in range(depth):
        ctx_out = i < depth - 1
        j = i // 2
        lat = [mods[i, :b, m * d:(m + 1) * d][:, None, :] for m in range(6)]
        cxm = [jnp.broadcast_to(mods[i, b, m * d:(m + 1) * d][None, None, :], (b, 1, d)) for m in range(6)]
        gains = [norm_g[i, m][None, :] for m in range(4)]

        if i % 2 == 0:
            ws_pairs = sgu_w[j].astype(BF16).reshape(SGU_GROUPS // 2, 2 * SGU_CHUNK, SGU_CHUNK)
            bs_tab = jnp.repeat(sgu_b[j].T, (d - NA_WIDTH) // SGU_GROUPS, axis=1)
            bias = _na_bias_tables(na_rpb[j])
            qkv_c, u_c, vn_c = _hyb_in(xc, cxm[0], cxm[1], gains[0], hyb_in_bf, j)
            qkv_l, u_l, vn_l = _hyb_in(x, lat[0], lat[1], gains[0], hyb_in_bf, j)
            a_l = _na_attention(qkv_l, qkv_c, bias)
            x = _hyb_out(a_l, u_l, vn_l, ws_pairs, bs_tab, hyb_out_bf, j, x, lat[2], gains[1])
            if ctx_out:
                a_c = _ctx_attention(qkv_c)
                xc = _hyb_out(a_c, u_c, vn_c, ws_pairs, bs_tab, hyb_out_bf, j, xc, cxm[2], gains[1])
        else:
            lg = ret_log_decay[j].astype(F32)
            gn = ret_gn_g[j][None, :]
            zero_state = jnp.zeros((b, RET_HEADS, dk, dv), F32)
            p_c = _ret_in(xc, cxm[0], cxm[1], gains[0], ret_in_bf, j, cos2, sin2, rope=False)
            p_l = _ret_in(x, lat[0], lat[1], gains[0], ret_in_bf, j, cos2, sin2, rope=True)
            on_c, sf, sb = _ret_scan(p_c, lg, zero_state, zero_state, dk, dv)
            on_l, _, _ = _ret_scan(p_l, lg, sf, sb, dk, dv)
            x = _ret_out(p_l, on_l, gn, ret_out_bf, j, x, lat[2], gains[1])
            if ctx_out:
                xc = _ret_out(p_c, on_c, gn, ret_out_bf, j, xc, cxm[2], gains[1])

        x = _conv_ffn(x, lat[3], lat[4], gains[2], *ffn_w, i, lat[5], gains[3])
        if ctx_out:
            xc = _conv_ffn(xc, cxm[3], cxm[4], gains[2], *ffn_w, i, cxm[5], gains[3])
    return x
```

```python
import functools

import jax
import jax.numpy as jnp
from jax import lax
from jax.experimental import pallas as pl
from jax.experimental.pallas import tpu as pltpu

F32 = jnp.float32
BF16 = jnp.bfloat16

GRID_W = 64
EPS = 1e-6
NEG_INF = -1e30
LOG2E = 1.4426950408889634

NA_HEADS = 8
NA_HEAD_DIM = 64
NA_WIDTH = NA_HEADS * NA_HEAD_DIM
NA_KH = 8
NA_KW = 16
NA_GROUP = 4
SGU_GROUPS = 8
SGU_CHUNK = 128

RET_HEADS = 8
RET_CHUNK = 256
ROPE_BASE = 10000.0

CONV_W = 3
FFN_CHUNK = 256
HALO = 16

V7X_VMEM_LIMIT_BYTES = 56 * 1024 * 1024

NT_DIMS = (((1,), (1,)), ((), ()))
TN_DIMS = (((0,), (0,)), ((), ()))


def _params(n_axes):
    return pltpu.CompilerParams(dimension_semantics=("arbitrary",) * n_axes,
                                vmem_limit_bytes=V7X_VMEM_LIMIT_BYTES)


def _resident(shape):
    zeros = (0,) * len(shape)
    return pl.BlockSpec(shape, lambda *_: zeros, pipeline_mode=pl.Buffered(1))


def _layer_resident(stacked_shape, layer):
    tail = (0,) * (len(stacked_shape) - 1)
    return pl.BlockSpec((None,) + tuple(stacked_shape[1:]), lambda *_: (layer,) + tail, pipeline_mode=pl.Buffered(1))


def _token_tile(n, want):
    tm = min(n, want)
    assert n % tm == 0
    return tm


def _rms(v, gain):
    return v * lax.rsqrt(jnp.mean(v * v, axis=-1, keepdims=True) + EPS) * gain


def _silu(v):
    return v * (1.0 / (1.0 + jnp.exp(-v)))


def _norm_mod(x, gain, shift, scale):
    return _rms(x, gain) * (1.0 + scale) + shift


def _ada_kernel(c_ref, w_ref, b_ref, o_ref):
    s = _silu(c_ref[...])
    o_ref[0] = jnp.dot(s, w_ref[0], preferred_element_type=F32) + b_ref[0]


def _ada_rows(cond, ada_w, ada_b):
    depth, d, n = ada_w.shape
    tn = 2048
    return pl.pallas_call(
        _ada_kernel,
        out_shape=jax.ShapeDtypeStruct((depth, 8, n), F32),
        grid=(depth, n // tn),
        in_specs=[pl.BlockSpec((8, d), lambda i, j: (0, 0)),
                  pl.BlockSpec((1, d, tn), lambda i, j: (i, 0, j)),
                  pl.BlockSpec((1, 1, tn), lambda i, j: (i, 0, j))],
        out_specs=pl.BlockSpec((1, 8, tn), lambda i, j: (i, 0, j)),
        compiler_params=_params(2),
        name="ada_rows",
    )(cond, ada_w, ada_b.reshape(depth, 1, n))


def _hyb_in_kernel(x_ref, sh_ref, sc_ref, g_ref, w_ref, qkv_ref, u_ref, vn_ref):
    hb = _norm_mod(x_ref[0], g_ref[...], sh_ref[0], sc_ref[0]).astype(BF16)
    w = NA_WIDTH
    for part in range(3):
        acc = jnp.dot(hb, w_ref[:, part * w:(part + 1) * w], preferred_element_type=F32)
        if part == 0:
            acc = acc * (NA_HEAD_DIM ** -0.5 * LOG2E)
        qkv_ref[0, :, part * w:(part + 1) * w] = acc.astype(BF16)
    u = jax.nn.gelu(jnp.dot(hb, w_ref[:, 3 * w:4 * w], preferred_element_type=F32))
    u_ref[0] = u.astype(BF16)
    g = jax.nn.gelu(jnp.dot(hb, w_ref[:, 4 * w:5 * w], preferred_element_type=F32))
    mu = jnp.mean(g, axis=-1, keepdims=True)
    gc = g - mu
    var = jnp.mean(gc * gc, axis=-1, keepdims=True)
    vn_ref[0] = (gc * lax.rsqrt(var + EPS)).astype(BF16)


def _hyb_in(x, shift, scale, gain, w_bf, layer):
    b, n, d = x.shape
    tm = _token_tile(n, 1024)
    w = NA_WIDTH
    row = lambda bi, i: (bi, 0, 0)
    tile = lambda bi, i: (bi, i, 0)
    return pl.pallas_call(
        _hyb_in_kernel,
        out_shape=(jax.ShapeDtypeStruct((b, n, 3 * w), BF16),
                   jax.ShapeDtypeStruct((b, n, w), BF16),
                   jax.ShapeDtypeStruct((b, n, w), BF16)),
        grid=(b, n // tm),
        in_specs=[pl.BlockSpec((1, tm, d), tile),
                  pl.BlockSpec((1, 1, d), row),
                  pl.BlockSpec((1, 1, d), row),
                  _resident((1, d)),
                  _layer_resident(w_bf.shape, layer)],
        out_specs=(pl.BlockSpec((1, tm, 3 * w), tile),
                   pl.BlockSpec((1, tm, w), tile),
                   pl.BlockSpec((1, tm, w), tile)),
        compiler_params=_params(2),
        name="hyb_in",
    )(x, shift, scale, gain, w_bf)


def _split_heads(qg):
    lane = lax.broadcasted_iota(jnp.int32, qg.shape, 1)
    zero = jnp.zeros_like(qg)
    return jnp.concatenate([jnp.where(lane // NA_HEAD_DIM == hd, qg, zero) for hd in range(NA_GROUP)], axis=0)


def _merge_heads(og):
    m = og.shape[0] // NA_GROUP
    lane = lax.broadcasted_iota(jnp.int32, (m, og.shape[1]), 1)
    out = og[:m]
    for hd in range(1, NA_GROUP):
        out = jnp.where(lane // NA_HEAD_DIM == hd, og[hd * m:(hd + 1) * m], out)
    return out


def _fold_lanes(v, op, width=128):
    out = v[:, :width]
    for c0 in range(width, v.shape[1], width):
        out = op(out, v[:, c0:c0 + width])
    return out


def _na_kernel(q_ref, k_ref, v_ref, kc_ref, vc_ref, bias_ref, o_ref, snb0_ref, scx0_ref, snb1_ref, scx1_ref, *, rows):
    kh = NA_KH
    span = kh * GRID_W
    bufs = ((snb0_ref, scx0_ref), (snb1_ref, scx1_ref))

    def window(r):
        r0 = jnp.clip(r - kh // 2, 0, rows - kh)
        return r0, pl.multiple_of(r * GRID_W, GRID_W), pl.multiple_of(r0 * GRID_W, GRID_W)

    def scores(r, buf):
        snb_ref, scx_ref = buf
        r0, q_off, k_off = window(r)
        qs = _split_heads(q_ref[0, pl.ds(q_off, GRID_W), :])
        s = lax.dot_general(qs, k_ref[0, pl.ds(k_off, span), :], NT_DIMS, preferred_element_type=F32)
        dr0 = kh - 1 - (r - r0)
        base = jnp.where(dr0 % 2 == 0, dr0 // 2, kh + (dr0 - 1) // 2)
        for jj in range(kh // 2):
            lanes = slice(jj * 2 * GRID_W, (jj + 1) * 2 * GRID_W)
            snb_ref[:, lanes] = s[:, lanes] + bias_ref[0, base + jj]
        scx_ref[...] = lax.dot_general(qs, kc_ref[0], NT_DIMS, preferred_element_type=F32)

    def attend(r, buf):
        snb_ref, scx_ref = buf
        _, q_off, k_off = window(r)
        s_nb = snb_ref[...]
        s_cx = scx_ref[...]
        m = jnp.max(jnp.maximum(_fold_lanes(s_nb, jnp.maximum), _fold_lanes(s_cx, jnp.maximum)),
                    axis=-1, keepdims=True)
        p_nb = jnp.exp2(s_nb - m)
        p_cx = jnp.exp2(s_cx - m)
        denom = jnp.sum(_fold_lanes(p_nb, jnp.add) + _fold_lanes(p_cx, jnp.add), axis=-1, keepdims=True)
        o2 = (jnp.dot(p_nb.astype(BF16), v_ref[0, pl.ds(k_off, span), :], preferred_element_type=F32)
              + jnp.dot(p_cx.astype(BF16), vc_ref[0], preferred_element_type=F32)) / denom
        o_ref[0, pl.ds(q_off, GRID_W), :] = _merge_heads(o2).astype(BF16)

    scores(0, bufs[0])

    def pair_body(j, carry):
        r = 2 * j
        scores(r + 1, bufs[1])
        attend(r, bufs[0])
        scores(jnp.minimum(r + 2, rows - 1), bufs[0])
        attend(r + 1, bufs[1])
        return carry

    assert rows % 2 == 0
    lax.fori_loop(0, rows // 2, pair_body, 0, unroll=4)


def _na_bias_tables(rpb):
    h = rpb.shape[0]
    w, kw = GRID_W, NA_KW
    cidx = jnp.arange(w)
    c0 = jnp.clip(cidx - kw // 2, 0, w - kw)
    in_win = (cidx[None, :] >= c0[:, None]) & (cidx[None, :] < c0[:, None] + kw)
    edge = w - kw
    ext = jnp.concatenate([jnp.broadcast_to(rpb[..., :1], rpb.shape[:-1] + (edge,)), rpb.astype(F32),
                           jnp.broadcast_to(rpb[..., -1:], rpb.shape[:-1] + (edge,))], axis=-1)
    ext = jnp.concatenate([ext, jnp.zeros_like(ext[..., :1])], axis=-1)
    flow = jnp.tile(ext, (1, 1, w))[..., :w * (2 * w - 1)].reshape(ext.shape[:-1] + (w, 2 * w - 1))
    toep = flow[..., w - 1:]
    toep = jnp.where(in_win, toep, NEG_INF) * LOG2E
    g = NA_GROUP
    n_dr = toep.shape[1]
    tg = toep.reshape(h // g, g, n_dr, w, w).transpose(0, 2, 1, 3, 4).reshape(h // g, n_dr, g * w, w)
    tg = jnp.concatenate([tg, jnp.zeros_like(tg[:, :1])], axis=1)
    even = jnp.concatenate([tg[:, 0::2], tg[:, 1::2]], axis=-1)
    odd = jnp.concatenate([tg[:, 1:-1:2], tg[:, 2::2]], axis=-1)
    return jnp.concatenate([even, odd], axis=1)


def _na_attention(qkv, qkv_ctx, bias):
    b, t, _ = qkv.shape
    l = qkv_ctx.shape[1]
    rows = t // GRID_W
    assert rows >= NA_KH
    groups = NA_HEADS // NA_GROUP
    lanes = NA_GROUP * NA_HEAD_DIM
    return pl.pallas_call(
        functools.partial(_na_kernel, rows=rows),
        out_shape=jax.ShapeDtypeStruct((b, t, NA_WIDTH), BF16),
        grid=(b, groups),
        in_specs=[pl.BlockSpec((1, t, lanes), lambda bi, hg: (bi, 0, hg)),
                  pl.BlockSpec((1, t, lanes), lambda bi, hg: (bi, 0, groups + hg)),
                  pl.BlockSpec((1, t, lanes), lambda bi, hg: (bi, 0, 2 * groups + hg)),
                  pl.BlockSpec((1, l, lanes), lambda bi, hg: (bi, 0, groups + hg)),
                  pl.BlockSpec((1, l, lanes), lambda bi, hg: (bi, 0, 2 * groups + hg)),
                  pl.BlockSpec((1,) + bias.shape[1:], lambda bi, hg: (hg, 0, 0, 0))],
        out_specs=pl.BlockSpec((1, t, lanes), lambda bi, hg: (bi, 0, hg)),
        scratch_shapes=[pltpu.VMEM((NA_GROUP * GRID_W, NA_KH * GRID_W), F32),
                        pltpu.VMEM((NA_GROUP * GRID_W, l), F32)] * 2,
        compiler_params=_params(2),
        name="na_attention",
    )(qkv, qkv, qkv, qkv_ctx, qkv_ctx, bias)


def _ctx_attn_kernel(q_ref, k_ref, v_ref, o_ref):
    qs = _split_heads(q_ref[0])
    s = lax.dot_general(qs, k_ref[0], NT_DIMS, preferred_element_type=F32)
    p = jnp.exp2(s - jnp.max(s, axis=-1, keepdims=True))
    denom = jnp.sum(p, axis=-1, keepdims=True)
    o2 = jnp.dot(p.astype(BF16), v_ref[0], preferred_element_type=F32) / denom
    o_ref[0] = _merge_heads(o2).astype(BF16)


def _ctx_attention(qkv_ctx):
    b, l, _ = qkv_ctx.shape
    groups = NA_HEADS // NA_GROUP
    lanes = NA_GROUP * NA_HEAD_DIM
    return pl.pallas_call(
        _ctx_attn_kernel,
        out_shape=jax.ShapeDtypeStruct((b, l, NA_WIDTH), BF16),
        grid=(b, groups),
        in_specs=[pl.BlockSpec((1, l, lanes), lambda bi, hg: (bi, 0, hg)),
                  pl.BlockSpec((1, l, lanes), lambda bi, hg: (bi, 0, groups + hg)),
                  pl.BlockSpec((1, l, lanes), lambda bi, hg: (bi, 0, 2 * groups + hg))],
        out_specs=pl.BlockSpec((1, l, lanes), lambda bi, hg: (bi, 0, hg)),
        compiler_params=_params(2),
        name="ctx_attention",
    )(qkv_ctx, qkv_ctx, qkv_ctx)


def _hyb_out_kernel(a_ref, u_ref, vn_ref, ws_ref, bs_ref, w_ref, x_ref, g1_ref, ng_ref, o_ref, s_ref):
    tm = a_ref.shape[1]
    c = SGU_CHUNK
    gd = vn_ref.shape[2] // SGU_GROUPS
    for ci in range(tm // c):
        rows = slice(ci * c, (ci + 1) * c)
        for gp in range(SGU_GROUPS // 2):
            lanes = slice(gp * 2 * gd, (gp + 1) * 2 * gd)
            mixed2 = jnp.dot(ws_ref[gp], vn_ref[0, rows, lanes], preferred_element_type=F32)
            lane = lax.broadcasted_iota(jnp.int32, (c, 2 * gd), 1)
            mixed = jnp.where(lane < gd, mixed2[:c], mixed2[c:]) + bs_ref[:, lanes]
            s_ref[rows, lanes] = (u_ref[0, rows, lanes].astype(F32) * mixed).astype(BF16)
    wa = a_ref.shape[2]
    y = (jnp.dot(a_ref[0], w_ref[:wa, :], preferred_element_type=F32)
         + jnp.dot(s_ref[...], w_ref[wa:, :], preferred_element_type=F32))
    o_ref[0] = x_ref[0] + g1_ref[0] * _rms(y, ng_ref[...])


def _hyb_out(a, u, vn, ws_pairs, bs_tab, w_bf, layer, x, gate, gain):
    b, n, d = x.shape
    tm = _token_tile(n, 1024)
    w = a.shape[2]
    row = lambda bi, i: (bi, 0, 0)
    tile = lambda bi, i: (bi, i, 0)
    return pl.pallas_call(
        _hyb_out_kernel,
        out_shape=jax.ShapeDtypeStruct((b, n, d), F32),
        grid=(b, n // tm),
        in_specs=[pl.BlockSpec((1, tm, w), tile),
                  pl.BlockSpec((1, tm, w), tile),
                  pl.BlockSpec((1, tm, w), tile),
                  _resident(ws_pairs.shape),
                  _resident(bs_tab.shape),
                  _layer_resident(w_bf.shape, layer),
                  pl.BlockSpec((1, tm, d), tile),
                  pl.BlockSpec((1, 1, d), row),
                  _resident((1, d))],
        out_specs=pl.BlockSpec((1, tm, d), tile),
        scratch_shapes=[pltpu.VMEM((tm, w), BF16)],
        compiler_params=_params(2),
        name="hyb_out",
    )(a, u, vn, ws_pairs, bs_tab, w_bf, x, gate, gain)


def _ret_in_kernel(x_ref, sh_ref, sc_ref, g_ref, w_ref, cos_ref, sin_ref, o_ref, *, rope, dk):
    hb = _norm_mod(x_ref[0], g_ref[...], sh_ref[0], sc_ref[0]).astype(BF16)
    qk_w = RET_HEADS * dk
    n_out = o_ref.shape[2]
    step = 512
    for c0 in range(0, n_out, step):
        acc = jnp.dot(hb, w_ref[:, c0:c0 + step], preferred_element_type=F32)
        if c0 < 2 * qk_w:
            if c0 >= qk_w:
                acc = acc * (dk ** -0.5)
            if rope:
                parts = []
                for hd in range(step // dk):
                    t = acc[:, hd * dk:(hd + 1) * dk]
                    parts.append(t * cos_ref[...] + pltpu.roll(t, dk // 2, 1) * sin_ref[...])
                acc = jnp.concatenate(parts, axis=-1)
        elif c0 >= 2 * qk_w + (n_out - 2 * qk_w) // 2:
            acc = _silu(acc)
        o_ref[0, :, c0:c0 + step] = acc.astype(BF16)


def _ret_in(x, shift, scale, gain, w_bf, layer, cos2, sin2, rope):
    b, n, d = x.shape
    tm = _token_tile(n, 1024)
    n_out = w_bf.shape[2]
    dk = cos2.shape[1]
    row = lambda bi, i: (bi, 0, 0)
    tile = lambda bi, i: (bi, i, 0)
    return pl.pallas_call(
        functools.partial(_ret_in_kernel, rope=rope, dk=dk),
        out_shape=jax.ShapeDtypeStruct((b, n, n_out), BF16),
        grid=(b, n // tm),
        in_specs=[pl.BlockSpec((1, tm, d), tile),
                  pl.BlockSpec((1, 1, d), row),
                  pl.BlockSpec((1, 1, d), row),
                  _resident((1, d)),
                  _layer_resident(w_bf.shape, layer),
                  pl.BlockSpec((tm, dk), lambda bi, i: (i, 0)),
                  pl.BlockSpec((tm, dk), lambda bi, i: (i, 0))],
        out_specs=pl.BlockSpec((1, tm, n_out), tile),
        compiler_params=_params(2),
        name="ret_in",
    )(x, shift, scale, gain, w_bf, cos2, sin2)


def _ret_scan_kernel(lg_ref, q_ref, k_ref, v_ref, s0f_ref, s0b_ref, on_ref, sf_ref, sb_ref, pre_ref):
    head = pl.program_id(1)
    n, dv = v_ref.shape[1], v_ref.shape[2]
    dk = q_ref.shape[2]
    c = RET_CHUNK
    nc = n // c
    lgf = lg_ref[0, head]
    lgb = lg_ref[1, head]
    pos = lax.broadcasted_iota(jnp.int32, (c, 1), 0).astype(F32)
    diff = (lax.broadcasted_iota(jnp.int32, (c, c), 0) - lax.broadcasted_iota(jnp.int32, (c, c), 1)).astype(F32)
    lower, upper = diff >= 0, diff <= 0
    decay = (jnp.where(lower, jnp.exp(lgf * jnp.where(lower, diff, 0.0)), 0.0)
             + jnp.where(upper, jnp.exp(lgb * jnp.where(upper, -diff, 0.0)), 0.0))
    dq_f, dk_f, dc_f = jnp.exp(lgf * (pos + 1.0)), jnp.exp(lgf * (c - 1.0 - pos)), jnp.exp(lgf * c)
    dq_b, dk_b, dc_b = jnp.exp(lgb * (c - pos)), jnp.exp(lgb * pos), jnp.exp(lgb * c)

    sf_ref[0, 0] = s0f_ref[0, 0]
    sb_ref[0, 0] = s0b_ref[0, 0]

    def advance(ci, st_ref, rows, dk_row, dchunk):
        off = pl.multiple_of(ci * c, c)
        kd = (k_ref[0, pl.ds(off, c), :].astype(F32) * dk_row).astype(BF16)
        upd = lax.dot_general(kd, v_ref[0, pl.ds(off, c), :], TN_DIMS, preferred_element_type=F32)
        state = st_ref[0, 0]
        pre_ref[ci, rows, :] = state.astype(BF16)
        st_ref[0, 0] = state * dchunk + upd

    def state_body(i, carry):
        advance(i, sf_ref, slice(0, dk), dk_f, dc_f)
        advance(nc - 1 - i, sb_ref, slice(dk, 2 * dk), dk_b, dc_b)
        return carry

    lax.fori_loop(0, nc, state_body, 0, unroll=min(8, nc))

    def out_body(ci, carry):
        off = pl.multiple_of(ci * c, c)
        q = q_ref[0, pl.ds(off, c), :]
        k = k_ref[0, pl.ds(off, c), :]
        v = v_ref[0, pl.ds(off, c), :]
        s = lax.dot_general(q, k, NT_DIMS, preferred_element_type=F32) * decay
        qf = q.astype(F32)
        qq = jnp.concatenate([(qf * dq_f).astype(BF16), (qf * dq_b).astype(BF16)], axis=-1)
        tot = (jnp.dot(s.astype(BF16), v, preferred_element_type=F32)
               + jnp.dot(qq, pre_ref[ci], preferred_element_type=F32))
        mu = jnp.mean(tot, axis=-1, keepdims=True)
        tc = tot - mu
        var = jnp.mean(tc * tc, axis=-1, keepdims=True)
        on_ref[0, pl.ds(off, c), :] = (tc * lax.rsqrt(var + EPS)).astype(BF16)
        return carry

    lax.fori_loop(0, nc, out_body, 0, unroll=min(8, nc))


def _ret_scan(qkvg, log_decay, s0f, s0b, dk, dv):
    b, n, _ = qkvg.shape
    h = RET_HEADS
    v_blk0 = (2 * h * dk) // dv
    state = jax.ShapeDtypeStruct((b, h, dk, dv), F32)
    st_spec = pl.BlockSpec((1, 1, dk, dv), lambda bi, hi: (bi, hi, 0, 0))
    return pl.pallas_call(
        _ret_scan_kernel,
        out_shape=(jax.ShapeDtypeStruct((b, n, h * dv), BF16), state, state),
        grid=(b, h),
        in_specs=[pl.BlockSpec(memory_space=pltpu.SMEM),
                  pl.BlockSpec((1, n, dk), lambda bi, hi: (bi, 0, hi)),
                  pl.BlockSpec((1, n, dk), lambda bi, hi: (bi, 0, h + hi)),
                  pl.BlockSpec((1, n, dv), lambda bi, hi: (bi, 0, v_blk0 + hi)),
                  st_spec, st_spec],
        out_specs=(pl.BlockSpec((1, n, dv), lambda bi, hi: (bi, 0, hi)), st_spec, st_spec),
        scratch_shapes=[pltpu.VMEM((n // RET_CHUNK, 2 * dk, dv), BF16)],
        compiler_params=_params(2),
        name="ret_scan",
    )(log_decay, qkvg, qkvg, qkvg, s0f, s0b)


def _ret_out_kernel(gt_ref, on_ref, gn_ref, w_ref, x_ref, g1_ref, ng_ref, o_ref):
    z = (gt_ref[0].astype(F32) * (on_ref[0].astype(F32) * gn_ref[...])).astype(BF16)
    y = jnp.dot(z, w_ref[...], preferred_element_type=F32)
    o_ref[0] = x_ref[0] + g1_ref[0] * _rms(y, ng_ref[...])


def _ret_out(qkvg, on, gn_g, w_bf, layer, x, gate, gain):
    b, n, d = x.shape
    tm = _token_tile(n, 1024)
    vw = on.shape[2]
    g_blk = qkvg.shape[2] // vw - 1
    row = lambda bi, i: (bi, 0, 0)
    tile = lambda bi, i: (bi, i, 0)
    return pl.pallas_call(
        _ret_out_kernel,
        out_shape=jax.ShapeDtypeStruct((b, n, d), F32),
        grid=(b, n // tm),
        in_specs=[pl.BlockSpec((1, tm, vw), lambda bi, i: (bi, i, g_blk)),
                  pl.BlockSpec((1, tm, vw), tile),
                  _resident((1, vw)),
                  _layer_resident(w_bf.shape, layer),
                  pl.BlockSpec((1, tm, d), tile),
                  pl.BlockSpec((1, 1, d), row),
                  _resident((1, d))],
        out_specs=pl.BlockSpec((1, tm, d), tile),
        compiler_params=_params(2),
        name="ret_out",
    )(qkvg, on, gn_g, w_bf, x, gate, gain)


def _ffn_kernel(xp_ref, x_ref, xn_ref, sh_ref, sc_ref, g_ref, win_ref, cw_ref, cb_ref, wout_ref, g2_ref, ng_ref,
                o_ref, h_ref, a0_ref, a1_ref, acc_ref):
    i = pl.program_id(1)
    last = pl.num_programs(1) - 1
    tm = x_ref.shape[1]
    gain, shift, scale = g_ref[...], sh_ref[0], sc_ref[0]
    hp = jnp.where(i > 0, _norm_mod(xp_ref[0], gain, shift, scale), 0.0)
    hn = jnp.where(i < last, _norm_mod(xn_ref[0], gain, shift, scale), 0.0)
    h_ref[:HALO] = hp.astype(BF16)
    h_ref[HALO:HALO + tm] = _norm_mod(x_ref[0], gain, shift, scale).astype(BF16)
    h_ref[HALO + tm:] = hn.astype(BF16)
    fc = FFN_CHUNK
    ff = wout_ref.shape[0]
    nch = ff // fc

    def cols(ci, half):
        return pl.ds(pl.multiple_of(half * ff + ci * fc, fc), fc)

    def project(ci, a_ref):
        for half in range(2):
            a_ref[:, half * fc:(half + 1) * fc] = jnp.dot(h_ref[...], win_ref[:, cols(ci, half)],
                                                          preferred_element_type=F32)

    def conv(ci, a_ref, half):
        lanes = slice(half * fc, (half + 1) * fc)
        cw = cw_ref[:, cols(ci, half)]
        return (a_ref[pl.ds(HALO - 1, tm), lanes] * cw[0:1] + a_ref[pl.ds(HALO, tm), lanes] * cw[1:2]
                + a_ref[pl.ds(HALO + 1, tm), lanes] * cw[2:3] + cb_ref[:, cols(ci, half)])

    def mix(ci, a_ref):
        act = (_silu(conv(ci, a_ref, 0)) * conv(ci, a_ref, 1)).astype(BF16)
        acc_ref[...] += jnp.dot(act, wout_ref[pl.ds(pl.multiple_of(ci * fc, fc), fc), :],
                                preferred_element_type=F32)

    acc_ref[...] = jnp.zeros_like(acc_ref)
    project(0, a0_ref)

    def pair_body(j, carry):
        project(2 * j + 1, a1_ref)
        mix(2 * j, a0_ref)
        project(2 * j + 2, a0_ref)
        mix(2 * j + 1, a1_ref)
        return carry

    assert nch % 2 == 1
    lax.fori_loop(0, nch // 2, pair_body, 0)
    mix(nch - 1, a0_ref)
    o_ref[0] = x_ref[0] + g2_ref[0] * _rms(acc_ref[...], ng_ref[...])


def _conv_ffn(x, shift, scale, gain, win_bf, conv_w, conv_b, wout_bf, layer, gate, gain_out):
    b, n, d = x.shape
    tm = _token_tile(n, 1024)
    assert tm % HALO == 0
    ff = wout_bf.shape[1]
    assert ff % FFN_CHUNK == 0 and (ff // FFN_CHUNK) % 2 == 1
    hb = tm // HALO
    n_hb = n // HALO
    row = lambda bi, i: (bi, 0, 0)
    tile = lambda bi, i: (bi, i, 0)
    return pl.pallas_call(
        _ffn_kernel,
        out_shape=jax.ShapeDtypeStruct((b, n, d), F32),
        grid=(b, n // tm),
        in_specs=[pl.BlockSpec((1, HALO, d), lambda bi, i: (bi, jnp.maximum(i * hb - 1, 0), 0)),
                  pl.BlockSpec((1, tm, d), tile),
                  pl.BlockSpec((1, HALO, d), lambda bi, i: (bi, jnp.minimum((i + 1) * hb, n_hb - 1), 0)),
                  pl.BlockSpec((1, 1, d), row),
                  pl.BlockSpec((1, 1, d), row),
                  _resident((1, d)),
                  _layer_resident(win_bf.shape, layer),
                  _layer_resident(conv_w.shape, layer),
                  _layer_resident(conv_b.shape, layer),
                  _layer_resident(wout_bf.shape, layer),
                  pl.BlockSpec((1, 1, d), row),
                  _resident((1, d))],
        out_specs=pl.BlockSpec((1, tm, d), tile),
        scratch_shapes=[pltpu.VMEM((tm + 2 * HALO, d), BF16),
                        pltpu.VMEM((tm + 2 * HALO, 2 * FFN_CHUNK), F32),
                        pltpu.VMEM((tm + 2 * HALO, 2 * FFN_CHUNK), F32),
                        pltpu.VMEM((tm, d), F32)],
        compiler_params=_params(2),
        name="conv_ffn",
    )(x, x, x, shift, scale, gain, win_bf, conv_w, conv_b, wout_bf, gate, gain_out)


def _rope_tables(t, dk):
    tok = jnp.arange(t)
    row = (tok // GRID_W).astype(F32)
    col = (tok % GRID_W).astype(F32)
    n_freq = dk // 4
    inv = ROPE_BASE ** (-jnp.arange(n_freq, dtype=F32) / n_freq)
    ang = jnp.concatenate([row[:, None] * inv, col[:, None] * inv], axis=-1)
    cos, sin = jnp.cos(ang), jnp.sin(ang)
    return jnp.concatenate([cos, cos], axis=-1), jnp.concatenate([-sin, sin], axis=-1)


def kernel(x, c, ctx, c_ctx, ada_w, ada_b, norm_g, hyb_w_in, na_rpb, sgu_w, sgu_b, hyb_w_out, ret_w_in, ret_log_decay,
           ret_gn_g, ret_w_out, ffn_w_in, ffn_conv_w, ffn_conv_b, ffn_w_out):
    b, t, d = x.shape
    depth = ada_w.shape[0]
    dk = d // RET_HEADS
    dv = 2 * dk

    cond = jnp.zeros((8, d), F32).at[:b].set(c).at[b].set(c_ctx)
    mods = _ada_rows(cond, ada_w, ada_b)
    cos2, sin2 = _rope_tables(t, dk)
    xc = ctx
    hyb_in_bf, hyb_out_bf = hyb_w_in.astype(BF16), hyb_w_out.astype(BF16)
    ret_in_bf, ret_out_bf = ret_w_in.astype(BF16), ret_w_out.astype(BF16)
    ffn_w = (ffn_w_in.astype(BF16), ffn_conv_w, ffn_conv_b[:, None, :], ffn_w_out.astype(BF16))

    for i in range(depth):
        ctx_out = i < depth - 1
        j = i // 2
        lat = [mods[i, :b, m * d:(m + 1) * d][:, None, :] for m in range(6)]
        cxm = [jnp.broadcast_to(mods[i, b, m * d:(m + 1) * d][None, None, :], (b, 1, d)) for m in range(6)]
        gains = [norm_g[i, m][None, :] for m in range(4)]

        if i % 2 == 0:
            ws_pairs = sgu_w[j].astype(BF16).reshape(SGU_GROUPS // 2, 2 * SGU_CHUNK, SGU_CHUNK)
            bs_tab = jnp.repeat(sgu_b[j].T, (d - NA_WIDTH) // SGU_GROUPS, axis=1)
            bias = _na_bias_tables(na_rpb[j])
            qkv_c, u_c, vn_c = _hyb_in(xc, cxm[0], cxm[1], gains[0], hyb_in_bf, j)
            qkv_l, u_l, vn_l = _hyb_in(x, lat[0], lat[1], gains[0], hyb_in_bf, j)
            a_l = _na_attention(qkv_l, qkv_c, bias)
            x = _hyb_out(a_l, u_l, vn_l, ws_pairs, bs_tab, hyb_out_bf, j, x, lat[2], gains[1])
            if ctx_out:
                a_c = _ctx_attention(qkv_c)
                xc = _hyb_out(a_c, u_c, vn_c, ws_pairs, bs_tab, hyb_out_bf, j, xc, cxm[2], gains[1])
        else:
            lg = ret_log_decay[j].astype(F32)
            gn = ret_gn_g[j][None, :]
            zero_state = jnp.zeros((b, RET_HEADS, dk, dv), F32)
            p_c = _ret_in(xc, cxm[0], cxm[1], gains[0], ret_in_bf, j, cos2, sin2, rope=False)
            p_l = _ret_in(x, lat[0], lat[1], gains[0], ret_in_bf, j, cos2, sin2, rope=True)
            on_c, sf, sb = _ret_scan(p_c, lg, zero_state, zero_state, dk, dv)
            on_l, _, _ = _ret_scan(p_l, lg, sf, sb, dk, dv)
            x = _ret_out(p_l, on_l, gn, ret_out_bf, j, x, lat[2], gains[1])
            if ctx_out:
                xc = _ret_out(p_c, on_c, gn, ret_out_bf, j, xc, cxm[2], gains[1])

        x = _conv_ffn(x, lat[3], lat[4], gains[2], *ffn_w, i, lat[5], gains[3])
        if ctx_out:
            xc = _conv_ffn(xc, cxm[3], cxm[4], gains[2], *ffn_w, i, cxm[5], gains[3])
    return x
```
